```python
import math
import jax, jax.numpy as jnp
from jax import lax
import numpy as np

D_MODEL = 4096
BATCH = 32
SEQ = 256
DEPTH = 1
DEC_BATCH = 4
DEC_SEQ = 1024
PAST_LEN = 256

GRID_W = 64
N_HEADS = D_MODEL // 512
QK_DIM = 128
V_DIM = 2 * QK_DIM
D_ATT = N_HEADS * V_DIM
D_CONV = D_MODEL - D_ATT
D_MIX = D_ATT + D_CONV
PROJ_W = 3 * D_ATT + 3 * D_CONV
CONV_K = 3
D_FF = 4 * D_MODEL
N_MOD = 6
ROPE_AXIS_DIM = QK_DIM // 2
ROPE_THETA = 10000.0
Q_BLOCK = 128
EPS = 1e-6

kernel_name = 'hybrid_diffattn_shortconv_dit_step'


def rmsnorm(x, g):
    xf = x.astype(jnp.float32)
    y = xf * lax.rsqrt(jnp.mean(xf * xf, axis=-1, keepdims=True) + EPS)
    return (y * g.astype(jnp.float32)).astype(x.dtype)


def axial_angles(n_tokens):
    rows = n_tokens // GRID_W
    row = jnp.repeat(jnp.arange(rows, dtype=jnp.float32), GRID_W)
    col = jnp.tile(jnp.arange(GRID_W, dtype=jnp.float32), rows)
    inv = jnp.power(ROPE_THETA, -jnp.arange(0, ROPE_AXIS_DIM, 2, dtype=jnp.float32) / ROPE_AXIS_DIM)
    return row[:, None] * inv, col[:, None] * inv


def rotate_axis(xp, ang):
    cos = jnp.cos(ang)[:, None, None, :].astype(xp.dtype)
    sin = jnp.sin(ang)[:, None, None, :].astype(xp.dtype)
    x1, x2 = jnp.split(xp, 2, axis=-1)
    return jnp.concatenate([x1 * cos - x2 * sin, x1 * sin + x2 * cos], axis=-1)


def apply_axial_rope(x, row_ang, col_ang):
    return jnp.concatenate([rotate_axis(x[..., :ROPE_AXIS_DIM], row_ang),
                            rotate_axis(x[..., ROPE_AXIS_DIM:], col_ang)], axis=-1)


def modulation(cond, w_ada, b_ada):
    s = jax.nn.silu(cond) @ w_ada + b_ada
    return s.reshape(cond.shape[:-1] + (N_MOD, D_MODEL))


def project_heads(h, w_in, q_norm_g, k_norm_g):
    b, s, _ = h.shape
    proj = h @ w_in
    cuts = [D_ATT, 2 * D_ATT, 3 * D_ATT, 3 * D_ATT + D_CONV, 3 * D_ATT + 2 * D_CONV]
    q, k, v, b_gate, c_gate, u = jnp.split(proj, cuts, axis=-1)
    q = rmsnorm(q.reshape(b, s, N_HEADS, 2, QK_DIM), q_norm_g)
    k = rmsnorm(k.reshape(b, s, N_HEADS, 2, QK_DIM), k_norm_g)
    v = v.reshape(b, s, N_HEADS, V_DIM)
    return q, k, v, b_gate, c_gate, u


def short_conv(b_gate, c_gate, u, conv_w):
    z = c_gate * u
    zp = jnp.pad(z, ((0, 0), (1, 1), (0, 0)))
    y = conv_w[0] * zp[:, :-2] + conv_w[1] * zp[:, 1:-1] + conv_w[2] * zp[:, 2:]
    return b_gate * y


def diff_attention(q, k, v, lam, lam_init, subln_g):
    b, sq = q.shape[0], q.shape[1]
    nb = sq // Q_BLOCK
    qb = q.reshape(b, nb, Q_BLOCK, N_HEADS, 2, QK_DIM).swapaxes(0, 1)
    scale = QK_DIM ** -0.5

    def block(qi):
        s = jnp.einsum('bqhcd,bkhcd->bchqk', qi, k).astype(jnp.float32) * scale
        p = jax.nn.softmax(s, axis=-1)
        a = (p[:, 0] - lam * p[:, 1]).astype(v.dtype)
        return jnp.einsum('bhqk,bkhe->bqhe', a, v)

    o = lax.map(block, qb)
    o = o.swapaxes(0, 1).reshape(b, sq, N_HEADS, V_DIM)
    o = rmsnorm(o, subln_g) * (1.0 - lam_init)
    return o.reshape(b, sq, D_ATT)


def trunk_layer(x, mod, cached_k, cached_v, is_latent, layer_idx,
                norm_attn_g, w_in, q_norm_g, k_norm_g, lam_q1, lam_k1, lam_q2, lam_k2,
                subln_g, conv_w, w_out, norm_mlp_g, w_mlp_in, w_mlp_out):
    shift1, scale1, gate1, shift2, scale2, gate2 = [mod[:, None, i] for i in range(N_MOD)]
    h = rmsnorm(x, norm_attn_g) * (1.0 + scale1) + shift1
    q, k, v, b_gate, c_gate, u = project_heads(h, w_in, q_norm_g, k_norm_g)
    lam_init = 0.8 - 0.6 * math.exp(-0.3 * layer_idx)
    f32 = jnp.float32
    lam = (jnp.exp(jnp.sum(lam_q1.astype(f32) * lam_k1.astype(f32)))
           - jnp.exp(jnp.sum(lam_q2.astype(f32) * lam_k2.astype(f32))) + lam_init)
    if is_latent:
        row_ang, col_ang = axial_angles(x.shape[1])
        q = apply_axial_rope(q, row_ang, col_ang)
        k = apply_axial_rope(k, row_ang, col_ang)
        keys = jnp.concatenate([k, cached_k], axis=1)
        vals = jnp.concatenate([v, cached_v], axis=1)
    else:
        keys, vals = k, v
    attn = diff_attention(q, keys, vals, lam, lam_init, subln_g)
    conv = short_conv(b_gate, c_gate, u, conv_w)
    x = x + gate1 * (jnp.concatenate([attn, conv], axis=-1) @ w_out)
    h2 = rmsnorm(x, norm_mlp_g) * (1.0 + scale2) + shift2
    x = x + gate2 * (jnp.square(jax.nn.relu(h2 @ w_mlp_in)) @ w_mlp_out)
    return x, k, v


def setup_inputs(seed: int = 0) -> dict:
    key = jax.random.key(seed)
    ks = jax.random.split(key, 24)
    f32 = jnp.float32
    nrm = lambda k, shp, s: jax.random.normal(k, shp, f32) * s
    return {
        'x_prompt': nrm(ks[0], (BATCH, SEQ, D_MODEL), 1.0),
        'x_sample': nrm(ks[1], (DEC_BATCH, DEC_SEQ, D_MODEL), 1.0),
        'cache_k': nrm(ks[2], (DEC_BATCH, DEPTH, PAST_LEN, N_HEADS, 2, QK_DIM), 1.0),
        'cache_v': nrm(ks[3], (DEC_BATCH, DEPTH, PAST_LEN, N_HEADS, V_DIM), 1.0),
        'c': nrm(ks[4], (DEC_BATCH, D_MODEL), 1.0),
        'c_ctx': nrm(ks[5], (D_MODEL,), 1.0),
        'w_ada': nrm(ks[6], (DEPTH, D_MODEL, N_MOD * D_MODEL), D_MODEL ** -0.5),
        'b_ada': nrm(ks[7], (DEPTH, N_MOD * D_MODEL), 0.01),
        'norm_attn_g': 1.0 + nrm(ks[8], (DEPTH, D_MODEL), 0.02),
        'w_in': nrm(ks[9], (DEPTH, D_MODEL, PROJ_W), D_MODEL ** -0.5),
        'q_norm_g': 1.0 + nrm(ks[10], (DEPTH, QK_DIM), 0.02),
        'k_norm_g': 1.0 + nrm(ks[11], (DEPTH, QK_DIM), 0.02),
        'lambda_q1': nrm(ks[12], (DEPTH, QK_DIM), 0.1),
        'lambda_k1': nrm(ks[13], (DEPTH, QK_DIM), 0.1),
        'lambda_q2': nrm(ks[14], (DEPTH, QK_DIM), 0.1),
        'lambda_k2': nrm(ks[15], (DEPTH, QK_DIM), 0.1),
        'subln_g': 1.0 + nrm(ks[16], (DEPTH, V_DIM), 0.02),
        'conv_w': nrm(ks[17], (DEPTH, CONV_K, D_CONV), CONV_K ** -0.5),
        'w_out': nrm(ks[18], (DEPTH, D_MIX, D_MODEL), D_MIX ** -0.5),
        'norm_mlp_g': 1.0 + nrm(ks[19], (DEPTH, D_MODEL), 0.02),
        'w_mlp_in': nrm(ks[20], (DEPTH, D_MODEL, D_FF), D_MODEL ** -0.5),
        'w_mlp_out': nrm(ks[21], (DEPTH, D_FF, D_MODEL), D_FF ** -0.5),
    }


def reference(x_prompt, x_sample, cache_k, cache_v, c, c_ctx, w_ada, b_ada, norm_attn_g, w_in,
              q_norm_g, k_norm_g, lambda_q1, lambda_k1, lambda_q2, lambda_k2, subln_g, conv_w,
              w_out, norm_mlp_g, w_mlp_in, w_mlp_out):
    y_p = x_prompt
    y_s = x_sample
    ctx_keys = []
    ctx_vals = []
    for l in range(DEPTH):
        layer_w = (norm_attn_g[l], w_in[l], q_norm_g[l], k_norm_g[l], lambda_q1[l], lambda_k1[l],
                   lambda_q2[l], lambda_k2[l], subln_g[l], conv_w[l], w_out[l], norm_mlp_g[l],
                   w_mlp_in[l], w_mlp_out[l])
        mod_ctx = modulation(c_ctx[None], w_ada[l], b_ada[l])
        y_p, k_l, v_l = trunk_layer(y_p, mod_ctx, None, None, False, l, *layer_w)
        ctx_keys.append(k_l)
        ctx_vals.append(v_l)
        mod_lat = modulation(c, w_ada[l], b_ada[l])
        y_s, _, _ = trunk_layer(y_s, mod_lat, cache_k[:, l], cache_v[:, l], True, l, *layer_w)
    new_k = jnp.stack(ctx_keys, axis=1)
    new_v = jnp.stack(ctx_vals, axis=1)
    return (y_p, y_s, new_k, new_v)
```

```python
import functools
import math

import jax
import jax.numpy as jnp
from jax import lax
from jax.experimental import pallas as pl
from jax.experimental.pallas import tpu as pltpu

_GRID_W = 64
_ROPE_THETA = 10000.0
_EPS = 1e-6
_N_MOD = 6
_LANES = 128
_VMEM_LIMIT = 56 * 1024 * 1024

_F32 = jnp.float32
_BF16 = jnp.bfloat16


def _tile(dim, pref):
    t = min(dim, pref)
    assert dim % t == 0, (dim, pref)
    return t


def _params(*sem):
    return pltpu.CompilerParams(dimension_semantics=sem, vmem_limit_bytes=_VMEM_LIMIT)


def _mod_kernel(cond_ref, w_ref, b_ref, o_ref):
    c = cond_ref[...]
    s = c / (1.0 + jnp.exp(-c))
    o_ref[...] = jnp.dot(s, w_ref[...], preferred_element_type=_F32) + b_ref[...]


def _modulation(cond, w_ada, b_ada):
    r, d = cond.shape
    n = w_ada.shape[1]
    tn = _tile(n, 512)
    return pl.pallas_call(
        _mod_kernel,
        grid=(n // tn,),
        in_specs=[pl.BlockSpec((r, d), lambda j: (0, 0)),
                  pl.BlockSpec((d, tn), lambda j: (0, j)),
                  pl.BlockSpec((1, tn), lambda j: (0, j))],
        out_specs=pl.BlockSpec((r, tn), lambda j: (0, j)),
        out_shape=jax.ShapeDtypeStruct((r, n), _F32),
        compiler_params=_params("arbitrary"),
        name="modulation",
    )(cond, w_ada, b_ada.reshape(1, n))


def _prenorm_kernel(x_ref, g_ref, mod_ref, o_ref, *, shift_idx, scale_idx):
    x = x_ref[...]
    y = x * lax.rsqrt(jnp.mean(x * x, axis=-1, keepdims=True) + _EPS) * g_ref[...]
    scale = mod_ref[scale_idx:scale_idx + 1, :]
    shift = mod_ref[shift_idx:shift_idx + 1, :]
    o_ref[...] = (y * (1.0 + scale) + shift).astype(o_ref.dtype)


def _prenorm(x, g, mod3, shift_idx, scale_idx):
    r, d = x.shape
    rows_per_group = r // mod3.shape[0]
    tr = _tile(rows_per_group, 256)
    return pl.pallas_call(
        functools.partial(_prenorm_kernel, shift_idx=shift_idx, scale_idx=scale_idx),
        grid=(r // tr,),
        in_specs=[pl.BlockSpec((tr, d), lambda i: (i, 0)),
                  pl.BlockSpec((1, d), lambda i: (0, 0)),
                  pl.BlockSpec((None, _N_MOD, d), lambda i: ((i * tr) // rows_per_group, 0, 0))],
        out_specs=pl.BlockSpec((tr, d), lambda i: (i, 0)),
        out_shape=jax.ShapeDtypeStruct((r, d), _BF16),
        compiler_params=_params("parallel"),
        name="prenorm",
    )(x, g.reshape(1, d), mod3)


def _headnorm(xc, g, scale):
    r = lax.rsqrt(jnp.mean(xc * xc, axis=-1, keepdims=True) + _EPS)
    if scale != 1.0:
        r = r * scale
    return xc * r * g


def _rope(y, cos, sin_signed, first_half):
    qk = y.shape[-1]
    ahead = pltpu.roll(y, qk - qk // 4, axis=1)
    behind = pltpu.roll(y, qk // 4, axis=1)
    return y * cos + jnp.where(first_half, ahead, behind) * sin_signed


def _qkv_kernel(*refs, latent, qk, n_sec, q_scale):
    if latent:
        x_ref, w_ref, qg_ref, kg_ref, cos_ref, sin_ref, qkv_ref = refs
    else:
        x_ref, w_ref, qg_ref, kg_ref, qkv_ref, newk_ref, newv_ref = refs
    j = pl.program_id(1)
    acc = jnp.dot(x_ref[...], w_ref[...], preferred_element_type=_F32)
    tn = acc.shape[1]
    if latent:
        cos = cos_ref[...]
        sin = sin_ref[...]
        lane = lax.broadcasted_iota(jnp.int32, (1, qk), 1)
        first_half = (lane % (qk // 2)) < (qk // 4)

    def qk_epilogue(g_ref, scale, f32_ref):
        g = g_ref[...]
        for c in range(tn // qk):
            sl = slice(c * qk, (c + 1) * qk)
            y = _headnorm(acc[:, sl], g, scale)
            if latent:
                y = _rope(y, cos, sin, first_half)
            if f32_ref is not None:
                f32_ref[:, sl] = y
            qkv_ref[:, sl] = y.astype(qkv_ref.dtype)

    @pl.when(j < n_sec)
    def _():
        qk_epilogue(qg_ref, q_scale, None)

    @pl.when((j >= n_sec) & (j < 2 * n_sec))
    def _():
        qk_epilogue(kg_ref, 1.0, None if latent else newk_ref)

    @pl.when(j >= 2 * n_sec)
    def _():
        qkv_ref[...] = acc.astype(qkv_ref.dtype)
        if not latent:
            newv_ref[...] = acc


def _qkv_proj(h, w_in, q_g, k_g, d_att, qk, rope=None):
    r, d = h.shape
    latent = rope is not None
    tm = _tile(r if not latent else rope[0].shape[0], 1024)
    tn = _tile(d_att, 512)
    n_sec = d_att // tn
    in_specs = [pl.BlockSpec((tm, d), lambda i, j: (i, 0)),
                pl.BlockSpec((d, tn), lambda i, j: (0, j)),
                pl.BlockSpec((1, qk), lambda i, j: (0, 0)),
                pl.BlockSpec((1, qk), lambda i, j: (0, 0))]
    args = [h, w_in, q_g.reshape(1, qk), k_g.reshape(1, qk)]
    qkv_spec = pl.BlockSpec((tm, tn), lambda i, j: (i, j))
    qkv_shape = jax.ShapeDtypeStruct((r, 3 * d_att), _BF16)
    if latent:
        seq_tiles = rope[0].shape[0] // tm
        rope_spec = pl.BlockSpec((tm, qk), lambda i, j: (i % seq_tiles, 0))
        in_specs += [rope_spec, rope_spec]
        args += list(rope)
        out_specs, out_shape = qkv_spec, qkv_shape
    else:
        out_specs = [qkv_spec,
                     pl.BlockSpec((tm, tn), lambda i, j: (i, jnp.clip(j - n_sec, 0, n_sec - 1))),
                     pl.BlockSpec((tm, tn), lambda i, j: (i, jnp.clip(j - 2 * n_sec, 0, n_sec - 1)))]
        out_shape = [qkv_shape,
                     jax.ShapeDtypeStruct((r, d_att), _F32),
                     jax.ShapeDtypeStruct((r, d_att), _F32)]
    return pl.pallas_call(
        functools.partial(_qkv_kernel, latent=latent, qk=qk, n_sec=n_sec, q_scale=qk ** -0.5),
        grid=(r // tm, 3 * n_sec),
        in_specs=in_specs, out_specs=out_specs, out_shape=out_shape,
        compiler_params=_params("parallel", "arbitrary"),
        name="qkv_proj_latent" if latent else "qkv_proj_context",
    )(*args)


def _conv_kernel(x_ref, wb_ref, wc_ref, wu_ref, cw_ref, o_ref, *, seq):
    x = x_ref[...]
    b = jnp.dot(x, wb_ref[...], preferred_element_type=_F32)
    c = jnp.dot(x, wc_ref[...], preferred_element_type=_F32)
    u = jnp.dot(x, wu_ref[...], preferred_element_type=_F32)
    z = c * u
    tm = z.shape[0]
    pos = lax.broadcasted_iota(jnp.int32, (tm, 1), 0) % seq
    z_prev = jnp.where(pos == 0, 0.0, pltpu.roll(z, 1, axis=0))
    z_next = jnp.where(pos == seq - 1, 0.0, pltpu.roll(z, tm - 1, axis=0))
    cw = cw_ref[...]
    y = cw[0:1, :] * z_prev + cw[1:2, :] * z + cw[2:3, :] * z_next
    o_ref[...] = (b * y).astype(o_ref.dtype)


def _conv_proj(h, w_in, conv_w, d_att, seq):
    r, d = h.shape
    d_conv = conv_w.shape[1]
    tm = _tile(r, max(seq, 1024))
    assert tm % seq == 0
    tc = _tile(d_conv, 256)
    off = 3 * d_att // tc
    nc = d_conv // tc

    def w_spec(sec):
        return pl.BlockSpec((d, tc), lambda i, j: (0, off + sec * nc + j))

    return pl.pallas_call(
        functools.partial(_conv_kernel, seq=seq),
        grid=(r // tm, nc),
        in_specs=[pl.BlockSpec((tm, d), lambda i, j: (i, 0)),
                  w_spec(0), w_spec(1), w_spec(2),
                  pl.BlockSpec((conv_w.shape[0], tc), lambda i, j: (0, j))],
        out_specs=pl.BlockSpec((tm, tc), lambda i, j: (i, j)),
        out_shape=jax.ShapeDtypeStruct((r, d_conv), _BF16),
        compiler_params=_params("parallel", "arbitrary"),
        name="conv_proj",
    )(h, w_in, w_in, w_in, conv_w)


def _attn_kernel(*refs, n_heads, qk, lam_init, cached):
    if cached:
        q_ref, k_ref, v_ref, ck_ref, cv_ref, lam_ref, g_ref, o_ref = refs
    else:
        q_ref, k_ref, v_ref, lam_ref, g_ref, o_ref = refs
    hw = 2 * qk
    lv = lam_ref[...]
    lam = (jnp.exp(jnp.sum(lv[0:1] * lv[1:2], axis=-1, keepdims=True))
           - jnp.exp(jnp.sum(lv[2:3] * lv[3:4], axis=-1, keepdims=True)) + lam_init)
    g = g_ref[...]
    dims = (((1,), (1,)), ((), ()))

    def probs(q, keys):
        s = [lax.dot_general(q, k, dims, preferred_element_type=_F32) for k in keys]
        m = functools.reduce(jnp.maximum, [jnp.max(x, axis=-1, keepdims=True) for x in s])
        e = [jnp.exp(x - m) for x in s]
        den = functools.reduce(jnp.add, [jnp.sum(x, axis=-1, keepdims=True) for x in e])
        return e, 1.0 / den

    for h in range(n_heads):
        o = None
        c0, c1 = h * hw, h * hw + qk
        keys0 = [k_ref[:, c0:c0 + qk]]
        keys1 = [k_ref[:, c1:c1 + qk]]
        vals = [v_ref[:, c0:c0 + hw]]
        if cached:
            keys0.append(ck_ref[:, c0:c0 + qk].astype(_BF16))
            keys1.append(ck_ref[:, c1:c1 + qk].astype(_BF16))
            vals.append(cv_ref[:, c0:c0 + hw].astype(_BF16))
        e0, r0 = probs(q_ref[:, c0:c0 + qk], keys0)
        e1, r1 = probs(q_ref[:, c1:c1 + qk], keys1)
        r1 = r1 * lam
        for p0, p1, v in zip(e0, e1, vals):
            a = (p0 * r0 - p1 * r1).astype(_BF16)
            part = jnp.dot(a, v, preferred_element_type=_F32)
            o = part if o is None else o + part
        o = o * lax.rsqrt(jnp.mean(o * o, axis=-1, keepdims=True) + _EPS) * g * (1.0 - lam_init)
        o_ref[:, c0:c0 + hw] = o.astype(o_ref.dtype)


def _attention(qkv, lam_vecs, subln_g, n_heads, qk, seq, lam_init, cache=None):
    r = qkv.shape[0]
    d_att = n_heads * 2 * qk
    tq = _tile(seq, 256)
    q_tiles = seq // tq
    in_specs = [pl.BlockSpec((tq, d_att), lambda b, t: (b * q_tiles + t, 0)),
                pl.BlockSpec((seq, d_att), lambda b, t: (b, 1)),
                pl.BlockSpec((seq, d_att), lambda b, t: (b, 2))]
    args = [qkv, qkv, qkv]
    if cache is not None:
        past = cache[0].shape[1]
        c_spec = pl.BlockSpec((None, past, d_att), lambda b, t: (b, 0, 0))
        in_specs += [c_spec, c_spec]
        args += list(cache)
    in_specs += [pl.BlockSpec(lam_vecs.shape, lambda b, t: (0, 0)),
                 pl.BlockSpec((1, 2 * qk), lambda b, t: (0, 0))]
    args += [lam_vecs, subln_g.reshape(1, 2 * qk)]
    return pl.pallas_call(
        functools.partial(_attn_kernel, n_heads=n_heads, qk=qk, lam_init=lam_init,
                          cached=cache is not None),
        grid=(r // seq, q_tiles),
        in_specs=in_specs,
        out_specs=pl.BlockSpec((tq, d_att), lambda b, t: (b * q_tiles + t, 0)),
        out_shape=jax.ShapeDtypeStruct((r, d_att), _BF16),
        compiler_params=_params("parallel", "arbitrary"),
        name="diff_attention_latent" if cache is not None else "diff_attention_context",
    )(*args)


def _outproj_kernel(a_ref, c_ref, wa_ref, wc_ref, x_ref, mod_ref, o_ref, *, gate_idx):
    y = (jnp.dot(a_ref[...], wa_ref[...], preferred_element_type=_F32)
         + jnp.dot(c_ref[...], wc_ref[...], preferred_element_type=_F32))
    o_ref[...] = x_ref[...] + mod_ref[gate_idx:gate_idx + 1, :] * y


def _out_proj(attn, conv, w_out, x, mod3, gate_idx):
    r, d = x.shape
    d_att, d_conv = attn.shape[1], conv.shape[1]
    assert d_att == d_conv
    rows_per_group = r // mod3.shape[0]
    tm = _tile(rows_per_group, 1024)
    tn = _tile(d, 512)
    return pl.pallas_call(
        functools.partial(_outproj_kernel, gate_idx=gate_idx),
        grid=(r // tm, d // tn),
        in_specs=[pl.BlockSpec((tm, d_att), lambda i, j: (i, 0)),
                  pl.BlockSpec((tm, d_conv), lambda i, j: (i, 0)),
                  pl.BlockSpec((d_att, tn), lambda i, j: (0, j)),
                  pl.BlockSpec((d_conv, tn), lambda i, j: (1, j)),
                  pl.BlockSpec((tm, tn), lambda i, j: (i, j)),
                  pl.BlockSpec((None, _N_MOD, tn), lambda i, j: ((i * tm) // rows_per_group, 0, j))],
        out_specs=pl.BlockSpec((tm, tn), lambda i, j: (i, j)),
        out_shape=jax.ShapeDtypeStruct((r, d), _F32),
        compiler_params=_params("parallel", "arbitrary"),
        name="out_proj",
    )(attn, conv, w_out, w_out, x, mod3)


def _mlp_kernel(h_ref, w1_ref, w2_ref, x_ref, mod_ref, o_ref, *, gate_idx):
    f = pl.program_id(1)

    @pl.when(f == 0)
    def _():
        o_ref[...] = jnp.zeros_like(o_ref)

    hid = jnp.dot(h_ref[...], w1_ref[...], preferred_element_type=_F32)
    hid = jnp.square(jnp.maximum(hid, 0.0)).astype(_BF16)
    o_ref[...] += jnp.dot(hid, w2_ref[...], preferred_element_type=_F32)

    @pl.when(f == pl.num_programs(1) - 1)
    def _():
        o_ref[...] = x_ref[...] + mod_ref[gate_idx:gate_idx + 1, :] * o_ref[...]


def _mlp(h, w1, w2, x, mod3, gate_idx):
    r, d = x.shape
    d_ff = w1.shape[1]
    rows_per_group = r // mod3.shape[0]
    tm = _tile(rows_per_group, 512)
    tf = _tile(d_ff, 256)
    return pl.pallas_call(
        functools.partial(_mlp_kernel, gate_idx=gate_idx),
        grid=(r // tm, d_ff // tf),
        in_specs=[pl.BlockSpec((tm, d), lambda i, f: (i, 0)),
                  pl.BlockSpec((d, tf), lambda i, f: (0, f)),
                  pl.BlockSpec((tf, d), lambda i, f: (f, 0)),
                  pl.BlockSpec((tm, d), lambda i, f: (i, 0)),
                  pl.BlockSpec((None, _N_MOD, d), lambda i, f: ((i * tm) // rows_per_group, 0, 0))],
        out_specs=pl.BlockSpec((tm, d), lambda i, f: (i, 0)),
        out_shape=jax.ShapeDtypeStruct((r, d), _F32),
        compiler_params=_params("parallel", "arbitrary"),
        name="mlp",
    )(h, w1, w2, x, mod3)


def _rope_tables(n_tokens, qk):
    axis_dim = qk // 2
    rows = n_tokens // _GRID_W
    row = jnp.repeat(jnp.arange(rows, dtype=_F32), _GRID_W)
    col = jnp.tile(jnp.arange(_GRID_W, dtype=_F32), rows)
    inv = jnp.power(_ROPE_THETA, -jnp.arange(0, axis_dim, 2, dtype=_F32) / axis_dim)
    row_ang, col_ang = row[:, None] * inv, col[:, None] * inv
    cos = jnp.concatenate([jnp.cos(row_ang)] * 2 + [jnp.cos(col_ang)] * 2, axis=-1)
    sin = jnp.concatenate([-jnp.sin(row_ang), jnp.sin(row_ang),
                           -jnp.sin(col_ang), jnp.sin(col_ang)], axis=-1)
    return cos, sin


def _trunk(x, mod3, seq, w, n_heads, qk, lam_init, rope=None, cache=None):
    d_att = n_heads * 2 * qk
    h = _prenorm(x, w["norm_attn_g"], mod3, 0, 1)
    proj = _qkv_proj(h, w["w_in"], w["q_norm_g"], w["k_norm_g"], d_att, qk, rope)
    qkv = proj if rope is not None else proj[0]
    conv = _conv_proj(h, w["w_in"], w["conv_w"], d_att, seq)
    attn = _attention(qkv, w["lam_vecs"], w["subln_g"], n_heads, qk, seq, lam_init, cache)
    x2 = _out_proj(attn, conv, w["w_out"], x, mod3, 2)
    h2 = _prenorm(x2, w["norm_mlp_g"], mod3, 3, 4)
    y = _mlp(h2, w["w_mlp_in"], w["w_mlp_out"], x2, mod3, 5)
    return y, proj


def kernel(x_prompt, x_sample, cache_k, cache_v, c, c_ctx, w_ada, b_ada, norm_attn_g, w_in, q_norm_g, k_norm_g, lambda_q1, lambda_k1, lambda_q2, lambda_k2, subln_g, conv_w, w_out, norm_mlp_g, w_mlp_in, w_mlp_out):
    assert w_in.shape[0] == 1, "one trunk layer"
    batch, seq, d = x_prompt.shape
    dec_batch, dec_seq, _ = x_sample.shape
    past, n_heads, _, qk = cache_k.shape[2:]
    d_att = n_heads * 2 * qk
    layer = 0
    lam_init = 0.8 - 0.6 * math.exp(-0.3 * layer)

    w = {
        "norm_attn_g": norm_attn_g[layer], "q_norm_g": q_norm_g[layer], "k_norm_g": k_norm_g[layer],
        "subln_g": subln_g[layer], "conv_w": conv_w[layer], "norm_mlp_g": norm_mlp_g[layer],
        "lam_vecs": jnp.stack([lambda_q1[layer], lambda_k1[layer], lambda_q2[layer], lambda_k2[layer]]),
        "w_in": w_in[layer].astype(_BF16), "w_out": w_out[layer].astype(_BF16),
        "w_mlp_in": w_mlp_in[layer].astype(_BF16), "w_mlp_out": w_mlp_out[layer].astype(_BF16),
    }

    n_cond = 1 + dec_batch
    cond = jnp.concatenate([c_ctx[None], c, jnp.zeros((-n_cond % 8, d), _F32)], axis=0)
    mod = _modulation(cond, w_ada[layer], b_ada[layer]).reshape(cond.shape[0], _N_MOD, d)

    y_p, (_, new_k, new_v) = _trunk(x_prompt.reshape(batch * seq, d), mod[0:1], seq, w,
                                    n_heads, qk, lam_init)
    cache = (cache_k[:, layer].reshape(dec_batch, past, d_att),
             cache_v[:, layer].reshape(dec_batch, past, d_att))
    y_s, _ = _trunk(x_sample.reshape(dec_batch * dec_seq, d), mod[1:n_cond], dec_seq, w,
                    n_heads, qk, lam_init, rope=_rope_tables(dec_seq, qk), cache=cache)

    return (y_p.reshape(batch, seq, d),
            y_s.reshape(dec_batch, dec_seq, d),
            new_k.reshape(batch, 1, seq, n_heads, 2, qk),
            new_v.reshape(batch, 1, seq, n_heads, 2 * qk))
```

```python
import functools
import math

import jax
import jax.numpy as jnp
from jax import lax
from jax.experimental import pallas as pl
from jax.experimental.pallas import tpu as pltpu

_GRID_W = 64
_ROPE_THETA = 10000.0
_EPS = 1e-6
_N_MOD = 6
_LANES = 128
_VMEM_LIMIT = 56 * 1024 * 1024

_F32 = jnp.float32
_BF16 = jnp.bfloat16


def _tile(dim, pref):
    t = min(dim, pref)
    assert dim % t == 0, (dim, pref)
    return t


def _params(*sem):
    return pltpu.CompilerParams(dimension_semantics=sem, vmem_limit_bytes=_VMEM_LIMIT)


def _mod_kernel(cond_ref, w_ref, b_ref, o_ref):
    c = cond_ref[...]
    s = c / (1.0 + jnp.exp(-c))
    o_ref[...] = jnp.dot(s, w_ref[...], preferred_element_type=_F32) + b_ref[...]


def _modulation(cond, w_ada, b_ada):
    r, d = cond.shape
    n = w_ada.shape[1]
    tn = _tile(n, 512)
    return pl.pallas_call(
        _mod_kernel,
        grid=(n // tn,),
        in_specs=[pl.BlockSpec((r, d), lambda j: (0, 0)),
                  pl.BlockSpec((d, tn), lambda j: (0, j)),
                  pl.BlockSpec((1, tn), lambda j: (0, j))],
        out_specs=pl.BlockSpec((r, tn), lambda j: (0, j)),
        out_shape=jax.ShapeDtypeStruct((r, n), _F32),
        compiler_params=_params("arbitrary"),
        name="modulation",
    )(cond, w_ada, b_ada.reshape(1, n))


def _prenorm_kernel(x_ref, g_ref, mod_ref, o_ref, *, shift_idx, scale_idx):
    x = x_ref[...]
    y = x * lax.rsqrt(jnp.mean(x * x, axis=-1, keepdims=True) + _EPS) * g_ref[...]
    scale = mod_ref[scale_idx:scale_idx + 1, :]
    shift = mod_ref[shift_idx:shift_idx + 1, :]
    o_ref[...] = (y * (1.0 + scale) + shift).astype(o_ref.dtype)


def _prenorm(x, g, mod3, shift_idx, scale_idx):
    r, d = x.shape
    rows_per_group = r // mod3.shape[0]
    tr = _tile(rows_per_group, 256)
    return pl.pallas_call(
        functools.partial(_prenorm_kernel, shift_idx=shift_idx, scale_idx=scale_idx),
        grid=(r // tr,),
        in_specs=[pl.BlockSpec((tr, d), lambda i: (i, 0)),
                  pl.BlockSpec((1, d), lambda i: (0, 0)),
                  pl.BlockSpec((None, _N_MOD, d), lambda i: ((i * tr) // rows_per_group, 0, 0))],
        out_specs=pl.BlockSpec((tr, d), lambda i: (i, 0)),
        out_shape=jax.ShapeDtypeStruct((r, d), _BF16),
        compiler_params=_params("parallel"),
        name="prenorm",
    )(x, g.reshape(1, d), mod3)


def _headnorm(xc, g, scale):
    r = lax.rsqrt(jnp.mean(xc * xc, axis=-1, keepdims=True) + _EPS)
    if scale != 1.0:
        r = r * scale
    return xc * r * g


def _rope(y, cos, sin_signed, first_half):
    qk = y.shape[-1]
    ahead = pltpu.roll(y, qk - qk // 4, axis=1)
    behind = pltpu.roll(y, qk // 4, axis=1)
    return y * cos + jnp.where(first_half, ahead, behind) * sin_signed


def _qkv_kernel(*refs, latent, qk, n_sec, q_scale):
    if latent:
        x_ref, w_ref, qg_ref, kg_ref, cos_ref, sin_ref, qkv_ref = refs
    else:
        x_ref, w_ref, qg_ref, kg_ref, qkv_ref, newk_ref, newv_ref = refs
    j = pl.program_id(1)
    acc = jnp.dot(x_ref[...], w_ref[...], preferred_element_type=_F32)
    tn = acc.shape[1]
    if latent:
        cos = cos_ref[...]
        sin = sin_ref[...]
        lane = lax.broadcasted_iota(jnp.int32, (1, qk), 1)
        first_half = (lane % (qk // 2)) < (qk // 4)

    def qk_epilogue(g_ref, scale, f32_ref):
        g = g_ref[...]
        for c in range(tn // qk):
            sl = slice(c * qk, (c + 1) * qk)
            y = _headnorm(acc[:, sl], g, scale)
            if latent:
                y = _rope(y, cos, sin, first_half)
            if f32_ref is not None:
                f32_ref[:, sl] = y
            qkv_ref[:, sl] = y.astype(qkv_ref.dtype)

    @pl.when(j < n_sec)
    def _():
        qk_epilogue(qg_ref, q_scale, None)

    @pl.when((j >= n_sec) & (j < 2 * n_sec))
    def _():
        qk_epilogue(kg_ref, 1.0, None if latent else newk_ref)

    @pl.when(j >= 2 * n_sec)
    def _():
        qkv_ref[...] = acc.astype(qkv_ref.dtype)
        if not latent:
            newv_ref[...] = acc


def _qkv_proj(h, w_in, q_g, k_g, d_att, qk, rope=None):
    r, d = h.shape
    latent = rope is not None
    tm = _tile(r if not latent else rope[0].shape[0], 1024)
    tn = _tile(d_att, 512)
    n_sec = d_att // tn
    in_specs = [pl.BlockSpec((tm, d), lambda i, j: (i, 0)),
                pl.BlockSpec((d, tn), lambda i, j: (0, j)),
                pl.BlockSpec((1, qk), lambda i, j: (0, 0)),
                pl.BlockSpec((1, qk), lambda i, j: (0, 0))]
    args = [h, w_in, q_g.reshape(1, qk), k_g.reshape(1, qk)]
    qkv_spec = pl.BlockSpec((tm, tn), lambda i, j: (i, j))
    qkv_shape = jax.ShapeDtypeStruct((r, 3 * d_att), _BF16)
    if latent:
        seq_tiles = rope[0].shape[0] // tm
        rope_spec = pl.BlockSpec((tm, qk), lambda i, j: (i % seq_tiles, 0))
        in_specs += [rope_spec, rope_spec]
        args += list(rope)
        out_specs, out_shape = qkv_spec, qkv_shape
    else:
        out_specs = [qkv_spec,
                     pl.BlockSpec((tm, tn), lambda i, j: (i, jnp.clip(j - n_sec, 0, n_sec - 1))),
                     pl.BlockSpec((tm, tn), lambda i, j: (i, jnp.clip(j - 2 * n_sec, 0, n_sec - 1)))]
        out_shape = [qkv_shape,
                     jax.ShapeDtypeStruct((r, d_att), _F32),
                     jax.ShapeDtypeStruct((r, d_att), _F32)]
    return pl.pallas_call(
        functools.partial(_qkv_kernel, latent=latent, qk=qk, n_sec=n_sec, q_scale=qk ** -0.5),
        grid=(r // tm, 3 * n_sec),
        in_specs=in_specs, out_specs=out_specs, out_shape=out_shape,
        compiler_params=_params("parallel", "arbitrary"),
        name="qkv_proj_latent" if latent else "qkv_proj_context",
    )(*args)


def _conv_kernel(x_ref, wb_ref, wc_ref, wu_ref, cw_ref, o_ref, *, seq):
    x = x_ref[...]
    b = jnp.dot(x, wb_ref[...], preferred_element_type=_F32)
    c = jnp.dot(x, wc_ref[...], preferred_element_type=_F32)
    u = jnp.dot(x, wu_ref[...], preferred_element_type=_F32)
    z = c * u
    tm = z.shape[0]
    pos = lax.broadcasted_iota(jnp.int32, (tm, 1), 0) % seq
    z_prev = jnp.where(pos == 0, 0.0, pltpu.roll(z, 1, axis=0))
    z_next = jnp.where(pos == seq - 1, 0.0, pltpu.roll(z, tm - 1, axis=0))
    cw = cw_ref[...]
    y = cw[0:1, :] * z_prev + cw[1:2, :] * z + cw[2:3, :] * z_next
    o_ref[...] = (b * y).astype(o_ref.dtype)


def _conv_proj(h, w_in, conv_w, d_att, seq):
    r, d = h.shape
    d_conv = conv_w.shape[1]
    tm = _tile(r, max(seq, 1024))
    assert tm % seq == 0
    tc = _tile(d_conv, 256)
    off = 3 * d_att // tc
    nc = d_conv // tc

    def w_spec(sec):
        return pl.BlockSpec((d, tc), lambda i, j: (0, off + sec * nc + j))

    return pl.pallas_call(
        functools.partial(_conv_kernel, seq=seq),
        grid=(r // tm, nc),
        in_specs=[pl.BlockSpec((tm, d), lambda i, j: (i, 0)),
                  w_spec(0), w_spec(1), w_spec(2),
                  pl.BlockSpec((conv_w.shape[0], tc), lambda i, j: (0, j))],
        out_specs=pl.BlockSpec((tm, tc), lambda i, j: (i, j)),
        out_shape=jax.ShapeDtypeStruct((r, d_conv), _BF16),
        compiler_params=_params("parallel", "arbitrary"),
        name="conv_proj",
    )(h, w_in, w_in, w_in, conv_w)


def _attn_kernel(*refs, n_heads, qk, lam_init, cached):
    if cached:
        q_ref, k_ref, v_ref, ck_ref, cv_ref, lam_ref, g_ref, o_ref = refs
    else:
        q_ref, k_ref, v_ref, lam_ref, g_ref, o_ref = refs
    hw = 2 * qk
    lv = lam_ref[...]
    lam = (jnp.exp(jnp.sum(lv[0:1] * lv[1:2], axis=-1, keepdims=True))
           - jnp.exp(jnp.sum(lv[2:3] * lv[3:4], axis=-1, keepdims=True)) + lam_init)
    g = g_ref[...]
    dims = (((1,), (1,)), ((), ()))

    def probs(q, keys):
        s = [lax.dot_general(q, k, dims, preferred_element_type=_F32) for k in keys]
        m = functools.reduce(jnp.maximum, [jnp.max(x, axis=-1, keepdims=True) for x in s])
        e = [jnp.exp(x - m) for x in s]
        den = functools.reduce(jnp.add, [jnp.sum(x, axis=-1, keepdims=True) for x in e])
        return e, 1.0 / den

    for h in range(n_heads):
        o = None
        c0, c1 = h * hw, h * hw + qk
        keys0 = [k_ref[:, c0:c0 + qk]]
        keys1 = [k_ref[:, c1:c1 + qk]]
        vals = [v_ref[:, c0:c0 + hw]]
        if cached:
            keys0.append(ck_ref[:, c0:c0 + qk].astype(_BF16))
            keys1.append(ck_ref[:, c1:c1 + qk].astype(_BF16))
            vals.append(cv_ref[:, c0:c0 + hw].astype(_BF16))
        e0, r0 = probs(q_ref[:, c0:c0 + qk], keys0)
        e1, r1 = probs(q_ref[:, c1:c1 + qk], keys1)
        r1 = r1 * lam
        for p0, p1, v in zip(e0, e1, vals):
            a = (p0 * r0 - p1 * r1).astype(_BF16)
            part = jnp.dot(a, v, preferred_element_type=_F32)
            o = part if o is None else o + part
        o = o * lax.rsqrt(jnp.mean(o * o, axis=-1, keepdims=True) + _EPS) * g * (1.0 - lam_init)
        o_ref[:, c0:c0 + hw] = o.astype(o_ref.dtype)


def _attention(qkv, lam_vecs, subln_g, n_heads, qk, seq, lam_init, cache=None):
    r = qkv.shape[0]
    d_att = n_heads * 2 * qk
    tq = _tile(seq, 256)
    q_tiles = seq // tq
    in_specs = [pl.BlockSpec((tq, d_att), lambda b, t: (b * q_tiles + t, 0)),
                pl.BlockSpec((seq, d_att), lambda b, t: (b, 1)),
                pl.BlockSpec((seq, d_att), lambda b, t: (b, 2))]
    args = [qkv, qkv, qkv]
    if cache is not None:
        past = cache[0].shape[1]
        c_spec = pl.BlockSpec((None, past, d_att), lambda b, t: (b, 0, 0))
        in_specs += [c_spec, c_spec]
        args += list(cache)
    in_specs += [pl.BlockSpec(lam_vecs.shape, lambda b, t: (0, 0)),
                 pl.BlockSpec((1, 2 * qk), lambda b, t: (0, 0))]
    args += [lam_vecs, subln_g.reshape(1, 2 * qk)]
    return pl.pallas_call(
        functools.partial(_attn_kernel, n_heads=n_heads, qk=qk, lam_init=lam_init,
                          cached=cache is not None),
        grid=(r // seq, q_tiles),
        in_specs=in_specs,
        out_specs=pl.BlockSpec((tq, d_att), lambda b, t: (b * q_tiles + t, 0)),
        out_shape=jax.ShapeDtypeStruct((r, d_att), _BF16),
        compiler_params=_params("parallel", "arbitrary"),
        name="diff_attention_latent" if cache is not None else "diff_attention_context",
    )(*args)


def _outproj_kernel(a_ref, c_ref, wa_ref, wc_ref, x_ref, mod_ref, o_ref, *, gate_idx):
    y = (jnp.dot(a_ref[...], wa_ref[...], preferred_element_type=_F32)
         + jnp.dot(c_ref[...], wc_ref[...], preferred_element_type=_F32))
    o_ref[...] = x_ref[...] + mod_ref[gate_idx:gate_idx + 1, :] * y


def _out_proj(attn, conv, w_out, x, mod3, gate_idx):
    r, d = x.shape
    d_att, d_conv = attn.shape[1], conv.shape[1]
    assert d_att == d_conv
    rows_per_group = r // mod3.shape[0]
    tm = _tile(rows_per_group, 1024)
    tn = _tile(d, 512)
    return pl.pallas_call(
        functools.partial(_outproj_kernel, gate_idx=gate_idx),
        grid=(r // tm, d // tn),
        in_specs=[pl.BlockSpec((tm, d_att), lambda i, j: (i, 0)),
                  pl.BlockSpec((tm, d_conv), lambda i, j: (i, 0)),
                  pl.BlockSpec((d_att, tn), lambda i, j: (0, j)),
                  pl.BlockSpec((d_conv, tn), lambda i, j: (1, j)),
                  pl.BlockSpec((tm, tn), lambda i, j: (i, j)),
                  pl.BlockSpec((None, _N_MOD, tn), lambda i, j: ((i * tm) // rows_per_group, 0, j))],
        out_specs=pl.BlockSpec((tm, tn), lambda i, j: (i, j)),
        out_shape=jax.ShapeDtypeStruct((r, d), _F32),
        compiler_params=_params("parallel", "arbitrary"),
        name="out_proj",
    )(attn, conv, w_out, w_out, x, mod3)


def _mlp_kernel(h_ref, w1_ref, w2_ref, x_ref, mod_ref, o_ref, hid_ref, *, gate_idx, n_f, tf):
    s = pl.program_id(1)

    @pl.when(s < n_f)
    def _():
        hid = jnp.dot(h_ref[...], w1_ref[...], preferred_element_type=_F32)
        col = pl.multiple_of(s * tf, tf)
        hid_ref[:, pl.ds(col, tf)] = jnp.square(jnp.maximum(hid, 0.0)).astype(hid_ref.dtype)

    @pl.when(s >= n_f)
    def _():
        y = jnp.dot(hid_ref[...], w2_ref[...], preferred_element_type=_F32)
        o_ref[...] = x_ref[...] + mod_ref[gate_idx:gate_idx + 1, :] * y


def _mlp(h, w1, w2, x, mod3, gate_idx):
    r, d = x.shape
    d_ff = w1.shape[1]
    rows_per_group = r // mod3.shape[0]
    tm = _tile(rows_per_group, 512)
    tf = _tile(d_ff, 512)
    tn = _tile(d, 256)
    n_f = d_ff // tf

    def out_col(s):
        return jnp.maximum(s - n_f, 0)

    return pl.pallas_call(
        functools.partial(_mlp_kernel, gate_idx=gate_idx, n_f=n_f, tf=tf),
        grid=(r // tm, n_f + d // tn),
        in_specs=[pl.BlockSpec((tm, d), lambda i, s: (i, 0)),
                  pl.BlockSpec((d, tf), lambda i, s: (0, jnp.minimum(s, n_f - 1))),
                  pl.BlockSpec((d_ff, tn), lambda i, s: (0, out_col(s))),
                  pl.BlockSpec((tm, tn), lambda i, s: (i, out_col(s))),
                  pl.BlockSpec((None, _N_MOD, tn),
                               lambda i, s: ((i * tm) // rows_per_group, 0, out_col(s)))],
        out_specs=pl.BlockSpec((tm, tn), lambda i, s: (i, out_col(s))),
        out_shape=jax.ShapeDtypeStruct((r, d), _F32),
        scratch_shapes=[pltpu.VMEM((tm, d_ff), _BF16)],
        compiler_params=_params("parallel", "arbitrary"),
        name="mlp",
    )(h, w1, w2, x, mod3)


def _rope_tables(n_tokens, qk):
    axis_dim = qk // 2
    rows = n_tokens // _GRID_W
    row = jnp.repeat(jnp.arange(rows, dtype=_F32), _GRID_W)
    col = jnp.tile(jnp.arange(_GRID_W, dtype=_F32), rows)
    inv = jnp.power(_ROPE_THETA, -jnp.arange(0, axis_dim, 2, dtype=_F32) / axis_dim)
    row_ang, col_ang = row[:, None] * inv, col[:, None] * inv
    cos = jnp.concatenate([jnp.cos(row_ang)] * 2 + [jnp.cos(col_ang)] * 2, axis=-1)
    sin = jnp.concatenate([-jnp.sin(row_ang), jnp.sin(row_ang),
                           -jnp.sin(col_ang), jnp.sin(col_ang)], axis=-1)
    return cos, sin


def _trunk(x, mod3, seq, w, n_heads, qk, lam_init, rope=None, cache=None):
    d_att = n_heads * 2 * qk
    h = _prenorm(x, w["norm_attn_g"], mod3, 0, 1)
    proj = _qkv_proj(h, w["w_in"], w["q_norm_g"], w["k_norm_g"], d_att, qk, rope)
    qkv = proj if rope is not None else proj[0]
    conv = _conv_proj(h, w["w_in"], w["conv_w"], d_att, seq)
    attn = _attention(qkv, w["lam_vecs"], w["subln_g"], n_heads, qk, seq, lam_init, cache)
    x2 = _out_proj(attn, conv, w["w_out"], x, mod3, 2)
    h2 = _prenorm(x2, w["norm_mlp_g"], mod3, 3, 4)
    y = _mlp(h2, w["w_mlp_in"], w["w_mlp_out"], x2, mod3, 5)
    return y, proj


def kernel(x_prompt, x_sample, cache_k, cache_v, c, c_ctx, w_ada, b_ada, norm_attn_g, w_in, q_norm_g, k_norm_g, lambda_q1, lambda_k1, lambda_q2, lambda_k2, subln_g, conv_w, w_out, norm_mlp_g, w_mlp_in, w_mlp_out):
    assert w_in.shape[0] == 1, "one trunk layer"
    batch, seq, d = x_prompt.shape
    dec_batch, dec_seq, _ = x_sample.shape
    past, n_heads, _, qk = cache_k.shape[2:]
    d_att = n_heads * 2 * qk
    layer = 0
    lam_init = 0.8 - 0.6 * math.exp(-0.3 * layer)

    w = {
        "norm_attn_g": norm_attn_g[layer], "q_norm_g": q_norm_g[layer], "k_norm_g": k_norm_g[layer],
        "subln_g": subln_g[layer], "conv_w": conv_w[layer], "norm_mlp_g": norm_mlp_g[layer],
        "lam_vecs": jnp.stack([lambda_q1[layer], lambda_k1[layer], lambda_q2[layer], lambda_k2[layer]]),
        "w_in": w_in[layer].astype(_BF16), "w_out": w_out[layer].astype(_BF16),
        "w_mlp_in": w_mlp_in[layer].astype(_BF16), "w_mlp_out": w_mlp_out[layer].astype(_BF16),
    }

    n_cond = 1 + dec_batch
    cond = jnp.concatenate([c_ctx[None], c, jnp.zeros((-n_cond % 8, d), _F32)], axis=0)
    mod = _modulation(cond, w_ada[layer], b_ada[layer]).reshape(cond.shape[0], _N_MOD, d)

    y_p, (_, new_k, new_v) = _trunk(x_prompt.reshape(batch * seq, d), mod[0:1], seq, w,
                                    n_heads, qk, lam_init)
    cache = (cache_k[:, layer].reshape(dec_batch, past, d_att),
             cache_v[:, layer].reshape(dec_batch, past, d_att))
    y_s, _ = _trunk(x_sample.reshape(dec_batch * dec_seq, d), mod[1:n_cond], dec_seq, w,
                    n_heads, qk, lam_init, rope=_rope_tables(dec_seq, qk), cache=cache)

    return (y_p.reshape(batch, seq, d),
            y_s.reshape(dec_batch, dec_seq, d),
            new_k.reshape(batch, 1, seq, n_heads, 2, qk),
            new_v.reshape(batch, 1, seq, n_heads, 2 * qk))
```

```python
import functools
import math

import jax
import jax.numpy as jnp
from jax import lax
from jax.experimental import pallas as pl
from jax.experimental.pallas import tpu as pltpu

_GRID_W = 64
_ROPE_THETA = 10000.0
_EPS = 1e-6
_N_MOD = 6
_LANES = 128
_VMEM_LIMIT = 56 * 1024 * 1024

_F32 = jnp.float32
_BF16 = jnp.bfloat16


def _tile(dim, pref):
    t = min(dim, pref)
    assert dim % t == 0, (dim, pref)
    return t


def _params(*sem):
    return pltpu.CompilerParams(dimension_semantics=sem, vmem_limit_bytes=_VMEM_LIMIT)


class _SideCast:
    def __init__(self, src, n_i, n_j):
        rows, cols = src.shape
        n_blocks = 1
        while n_blocks * 2 <= n_i * n_j and rows % (n_blocks * 2) == 0 and rows // (n_blocks * 2) >= 16:
            n_blocks *= 2
        self.src, self.n_j, self.n_blocks, self.all_steps = src, n_j, n_blocks, n_blocks == n_i * n_j
        block = (rows // n_blocks, cols)
        self.spec = pl.BlockSpec(block, lambda i, j: (jnp.minimum(i * n_j + j, n_blocks - 1), 0))
        self.out_shape = jax.ShapeDtypeStruct(src.shape, _BF16)

    def run(self, src_ref, dst_ref):
        def cast():
            dst_ref[...] = src_ref[...].astype(dst_ref.dtype)
        if self.all_steps:
            cast()
        else:
            pl.when(pl.program_id(0) * self.n_j + pl.program_id(1) < self.n_blocks)(cast)


def _mod_kernel(cond_ref, w_ref, b_ref, o_ref):
    c = cond_ref[...]
    s = c / (1.0 + jnp.exp(-c))
    o_ref[...] = jnp.dot(s, w_ref[...], preferred_element_type=_F32) + b_ref[...]


def _modulation(cond, w_ada, b_ada):
    r, d = cond.shape
    n = w_ada.shape[1]
    tn = _tile(n, 512)
    return pl.pallas_call(
        _mod_kernel,
        grid=(n // tn,),
        in_specs=[pl.BlockSpec((r, d), lambda j: (0, 0)),
                  pl.BlockSpec((d, tn), lambda j: (0, j)),
                  pl.BlockSpec((1, tn), lambda j: (0, j))],
        out_specs=pl.BlockSpec((r, tn), lambda j: (0, j)),
        out_shape=jax.ShapeDtypeStruct((r, n), _F32),
        compiler_params=_params("arbitrary"),
        name="modulation",
    )(cond, w_ada, b_ada.reshape(1, n))


def _prenorm_kernel(x_ref, g_ref, mod_ref, o_ref, *, shift_idx, scale_idx):
    x = x_ref[...]
    y = x * lax.rsqrt(jnp.mean(x * x, axis=-1, keepdims=True) + _EPS) * g_ref[...]
    scale = mod_ref[scale_idx:scale_idx + 1, :]
    shift = mod_ref[shift_idx:shift_idx + 1, :]
    o_ref[...] = (y * (1.0 + scale) + shift).astype(o_ref.dtype)


def _prenorm(x, g, mod3, shift_idx, scale_idx):
    r, d = x.shape
    rows_per_group = r // mod3.shape[0]
    tr = _tile(rows_per_group, 256)
    return pl.pallas_call(
        functools.partial(_prenorm_kernel, shift_idx=shift_idx, scale_idx=scale_idx),
        grid=(r // tr,),
        in_specs=[pl.BlockSpec((tr, d), lambda i: (i, 0)),
                  pl.BlockSpec((1, d), lambda i: (0, 0)),
                  pl.BlockSpec((None, _N_MOD, d), lambda i: ((i * tr) // rows_per_group, 0, 0))],
        out_specs=pl.BlockSpec((tr, d), lambda i: (i, 0)),
        out_shape=jax.ShapeDtypeStruct((r, d), _BF16),
        compiler_params=_params("parallel"),
        name="prenorm",
    )(x, g.reshape(1, d), mod3)


def _headnorm(xc, g, scale):
    r = lax.rsqrt(jnp.mean(xc * xc, axis=-1, keepdims=True) + _EPS)
    if scale != 1.0:
        r = r * scale
    return xc * r * g


def _rope(y, cos, sin_signed, first_half):
    qk = y.shape[-1]
    ahead = pltpu.roll(y, qk - qk // 4, axis=1)
    behind = pltpu.roll(y, qk // 4, axis=1)
    return y * cos + jnp.where(first_half, ahead, behind) * sin_signed


def _qkv_kernel(*refs, latent, qk, n_sec, q_scale, side_cast):
    if latent:
        x_ref, w_ref, qg_ref, kg_ref, cos_ref, sin_ref, qkv_ref = refs
    else:
        x_ref, w_ref, qg_ref, kg_ref, src_ref, qkv_ref, newk_ref, newv_ref, dst_ref = refs
        side_cast.run(src_ref, dst_ref)
    j = pl.program_id(1)
    tn = w_ref.shape[1]

    def project():
        return jnp.dot(x_ref[...], w_ref[...], preferred_element_type=_F32)

    if latent:
        shared = project()
        project = lambda: shared

    def qk_epilogue(g_ref, scale, f32_ref):
        acc = project()
        g = g_ref[...]
        if latent:
            cos = cos_ref[...]
            sin = sin_ref[...]
            lane = lax.broadcasted_iota(jnp.int32, (1, qk), 1)
            first_half = (lane % (qk // 2)) < (qk // 4)
        for c in range(tn // qk):
            cols = slice(c * qk, (c + 1) * qk)
            y = _headnorm(acc[:, cols], g, scale)
            if latent:
                y = _rope(y, cos, sin, first_half)
            if f32_ref is not None:
                f32_ref[:, cols] = y
            qkv_ref[:, cols] = y.astype(qkv_ref.dtype)

    @pl.when(j < n_sec)
    def _():
        qk_epilogue(qg_ref, q_scale, None)

    @pl.when((j >= n_sec) & (j < 2 * n_sec))
    def _():
        qk_epilogue(kg_ref, 1.0, None if latent else newk_ref)

    @pl.when(j >= 2 * n_sec)
    def _():
        acc = project()
        qkv_ref[...] = acc.astype(qkv_ref.dtype)
        if not latent:
            newv_ref[...] = acc


def _qkv_proj(h, w_in, q_g, k_g, d_att, qk, rope=None, cast_src=None):
    r, d = h.shape
    latent = rope is not None
    tm = _tile(r if not latent else rope[0].shape[0], 1024)
    tn = _tile(d_att, 512)
    n_sec = d_att // tn
    in_specs = [pl.BlockSpec((tm, d), lambda i, j: (i, 0)),
                pl.BlockSpec((d, tn), lambda i, j: (0, j)),
                pl.BlockSpec((1, qk), lambda i, j: (0, 0)),
                pl.BlockSpec((1, qk), lambda i, j: (0, 0))]
    args = [h, w_in, q_g.reshape(1, qk), k_g.reshape(1, qk)]
    qkv_spec = pl.BlockSpec((tm, tn), lambda i, j: (i, j))
    qkv_shape = jax.ShapeDtypeStruct((r, 3 * d_att), _BF16)
    if latent:
        seq_tiles = rope[0].shape[0] // tm
        rope_spec = pl.BlockSpec((tm, qk), lambda i, j: (i % seq_tiles, 0))
        in_specs += [rope_spec, rope_spec]
        args += list(rope)
        out_specs, out_shape = qkv_spec, qkv_shape
        side_cast = None
    else:
        side_cast = _SideCast(cast_src, r // tm, 3 * n_sec)
        in_specs.append(side_cast.spec)
        args.append(cast_src)
        out_specs = [qkv_spec,
                     pl.BlockSpec((tm, tn), lambda i, j: (i, jnp.clip(j - n_sec, 0, n_sec - 1))),
                     pl.BlockSpec((tm, tn), lambda i, j: (i, jnp.clip(j - 2 * n_sec, 0, n_sec - 1))),
                     side_cast.spec]
        out_shape = [qkv_shape,
                     jax.ShapeDtypeStruct((r, d_att), _F32),
                     jax.ShapeDtypeStruct((r, d_att), _F32),
                     side_cast.out_shape]
    return pl.pallas_call(
        functools.partial(_qkv_kernel, latent=latent, qk=qk, n_sec=n_sec, q_scale=qk ** -0.5,
                          side_cast=side_cast),
        grid=(r // tm, 3 * n_sec),
        in_specs=in_specs, out_specs=out_specs, out_shape=out_shape,
        compiler_params=_params("parallel", "arbitrary"),
        name="qkv_proj_latent" if latent else "qkv_proj_context",
    )(*args)


def _conv_kernel(x_ref, wb_ref, wc_ref, wu_ref, cw_ref, *rest, seq, side_cast):
    if side_cast is None:
        o_ref, = rest
    else:
        src_ref, o_ref, dst_ref = rest
        side_cast.run(src_ref, dst_ref)
    x = x_ref[...]
    b = jnp.dot(x, wb_ref[...], preferred_element_type=_F32)
    c = jnp.dot(x, wc_ref[...], preferred_element_type=_F32)
    u = jnp.dot(x, wu_ref[...], preferred_element_type=_F32)
    z = c * u
    tm = z.shape[0]
    pos = lax.broadcasted_iota(jnp.int32, (tm, 1), 0) % seq
    z_prev = jnp.where(pos == 0, 0.0, pltpu.roll(z, 1, axis=0))
    z_next = jnp.where(pos == seq - 1, 0.0, pltpu.roll(z, tm - 1, axis=0))
    cw = cw_ref[...]
    y = cw[0:1, :] * z_prev + cw[1:2, :] * z + cw[2:3, :] * z_next
    o_ref[...] = (b * y).astype(o_ref.dtype)


def _conv_proj(h, w_in, conv_w, d_att, seq, cast_src=None):
    r, d = h.shape
    d_conv = conv_w.shape[1]
    tm = _tile(r, max(seq, 1024))
    assert tm % seq == 0
    tc = _tile(d_conv, 256)
    off = 3 * d_att // tc
    nc = d_conv // tc

    def w_spec(sec):
        return pl.BlockSpec((d, tc), lambda i, j: (0, off + sec * nc + j))

    in_specs = [pl.BlockSpec((tm, d), lambda i, j: (i, 0)),
                w_spec(0), w_spec(1), w_spec(2),
                pl.BlockSpec((conv_w.shape[0], tc), lambda i, j: (0, j))]
    args = [h, w_in, w_in, w_in, conv_w]
    out_specs = pl.BlockSpec((tm, tc), lambda i, j: (i, j))
    out_shape = jax.ShapeDtypeStruct((r, d_conv), _BF16)
    side_cast = None
    if cast_src is not None:
        side_cast = _SideCast(cast_src, r // tm, nc)
        in_specs.append(side_cast.spec)
        args.append(cast_src)
        out_specs, out_shape = [out_specs, side_cast.spec], [out_shape, side_cast.out_shape]
    return pl.pallas_call(
        functools.partial(_conv_kernel, seq=seq, side_cast=side_cast),
        grid=(r // tm, nc),
        in_specs=in_specs, out_specs=out_specs, out_shape=out_shape,
        compiler_params=_params("parallel", "arbitrary"),
        name="conv_proj",
    )(*args)


def _attn_kernel(*refs, n_heads, qk, lam_init, cached):
    if cached:
        q_ref, k_ref, v_ref, ck_ref, cv_ref, lam_ref, g_ref, o_ref = refs
    else:
        q_ref, k_ref, v_ref, lam_ref, g_ref, o_ref = refs
    hw = 2 * qk
    lv = lam_ref[...]
    lam = (jnp.exp(jnp.sum(lv[0:1] * lv[1:2], axis=-1, keepdims=True))
           - jnp.exp(jnp.sum(lv[2:3] * lv[3:4], axis=-1, keepdims=True)) + lam_init)
    g = g_ref[...]
    dims = (((1,), (1,)), ((), ()))

    def probs(q, keys):
        s = [lax.dot_general(q, k, dims, preferred_element_type=_F32) for k in keys]
        m = functools.reduce(jnp.maximum, [jnp.max(x, axis=-1, keepdims=True) for x in s])
        e = [jnp.exp(x - m) for x in s]
        den = functools.reduce(jnp.add, [jnp.sum(x, axis=-1, keepdims=True) for x in e])
        return e, 1.0 / den

    for h in range(n_heads):
        o = None
        c0, c1 = h * hw, h * hw + qk
        keys0 = [k_ref[:, c0:c0 + qk]]
        keys1 = [k_ref[:, c1:c1 + qk]]
        vals = [v_ref[:, c0:c0 + hw]]
        if cached:
            keys0.append(ck_ref[:, c0:c0 + qk].astype(_BF16))
            keys1.append(ck_ref[:, c1:c1 + qk].astype(_BF16))
            vals.append(cv_ref[:, c0:c0 + hw].astype(_BF16))
        e0, r0 = probs(q_ref[:, c0:c0 + qk], keys0)
        e1, r1 = probs(q_ref[:, c1:c1 + qk], keys1)
        r1 = r1 * lam
        for p0, p1, v in zip(e0, e1, vals):
            a = (p0 * r0 - p1 * r1).astype(_BF16)
            part = jnp.dot(a, v, preferred_element_type=_F32)
            o = part if o is None else o + part
        o = o * lax.rsqrt(jnp.mean(o * o, axis=-1, keepdims=True) + _EPS) * g * (1.0 - lam_init)
        o_ref[:, c0:c0 + hw] = o.astype(o_ref.dtype)


def _attention(qkv, lam_vecs, subln_g, n_heads, qk, seq, lam_init, cache=None):
    r = qkv.shape[0]
    d_att = n_heads * 2 * qk
    tq = _tile(seq, 256)
    q_tiles = seq // tq
    in_specs = [pl.BlockSpec((tq, d_att), lambda b, t: (b * q_tiles + t, 0)),
                pl.BlockSpec((seq, d_att), lambda b, t: (b, 1)),
                pl.BlockSpec((seq, d_att), lambda b, t: (b, 2))]
    args = [qkv, qkv, qkv]
    if cache is not None:
        past = cache[0].shape[1]
        c_spec = pl.BlockSpec((None, past, d_att), lambda b, t: (b, 0, 0))
        in_specs += [c_spec, c_spec]
        args += list(cache)
    in_specs += [pl.BlockSpec(lam_vecs.shape, lambda b, t: (0, 0)),
                 pl.BlockSpec((1, 2 * qk), lambda b, t: (0, 0))]
    args += [lam_vecs, subln_g.reshape(1, 2 * qk)]
    return pl.pallas_call(
        functools.partial(_attn_kernel, n_heads=n_heads, qk=qk, lam_init=lam_init,
                          cached=cache is not None),
        grid=(r // seq, q_tiles),
        in_specs=in_specs,
        out_specs=pl.BlockSpec((tq, d_att), lambda b, t: (b * q_tiles + t, 0)),
        out_shape=jax.ShapeDtypeStruct((r, d_att), _BF16),
        compiler_params=_params("parallel", "arbitrary"),
        name="diff_attention_latent" if cache is not None else "diff_attention_context",
    )(*args)


def _outproj_kernel(a_ref, c_ref, wa_ref, wc_ref, x_ref, mod_ref, *rest, gate_idx, side_cast):
    if side_cast is None:
        o_ref, = rest
    else:
        src_ref, o_ref, dst_ref = rest
        side_cast.run(src_ref, dst_ref)
    y = (jnp.dot(a_ref[...], wa_ref[...], preferred_element_type=_F32)
         + jnp.dot(c_ref[...], wc_ref[...], preferred_element_type=_F32))
    o_ref[...] = x_ref[...] + mod_ref[gate_idx:gate_idx + 1, :] * y


def _out_proj(attn, conv, w_out, x, mod3, gate_idx, cast_src=None):
    r, d = x.shape
    d_att, d_conv = attn.shape[1], conv.shape[1]
    assert d_att == d_conv
    rows_per_group = r // mod3.shape[0]
    tm = _tile(rows_per_group, 1024)
    tn = _tile(d, 512)
    in_specs = [pl.BlockSpec((tm, d_att), lambda i, j: (i, 0)),
                pl.BlockSpec((tm, d_conv), lambda i, j: (i, 0)),
                pl.BlockSpec((d_att, tn), lambda i, j: (0, j)),
                pl.BlockSpec((d_conv, tn), lambda i, j: (1, j)),
                pl.BlockSpec((tm, tn), lambda i, j: (i, j)),
                pl.BlockSpec((None, _N_MOD, tn), lambda i, j: ((i * tm) // rows_per_group, 0, j))]
    args = [attn, conv, w_out, w_out, x, mod3]
    out_specs = pl.BlockSpec((tm, tn), lambda i, j: (i, j))
    out_shape = jax.ShapeDtypeStruct((r, d), _F32)
    side_cast = None
    if cast_src is not None:
        side_cast = _SideCast(cast_src, r // tm, d // tn)
        in_specs.append(side_cast.spec)
        args.append(cast_src)
        out_specs, out_shape = [out_specs, side_cast.spec], [out_shape, side_cast.out_shape]
    return pl.pallas_call(
        functools.partial(_outproj_kernel, gate_idx=gate_idx, side_cast=side_cast),
        grid=(r // tm, d // tn),
        in_specs=in_specs, out_specs=out_specs, out_shape=out_shape,
        compiler_params=_params("parallel", "arbitrary"),
        name="out_proj",
    )(*args)


def _mlp_kernel(h_ref, w1_ref, w2_ref, x_ref, mod_ref, o_ref, hid_ref, *, gate_idx, n_f, tf):
    s = pl.program_id(1)

    @pl.when(s < n_f)
    def _():
        hid = jnp.dot(h_ref[...], w1_ref[...], preferred_element_type=_F32)
        col = pl.multiple_of(s * tf, tf)
        hid_ref[:, pl.ds(col, tf)] = jnp.square(jnp.maximum(hid, 0.0)).astype(hid_ref.dtype)

    @pl.when(s >= n_f)
    def _():
        y = jnp.dot(hid_ref[...], w2_ref[...], preferred_element_type=_F32)
        o_ref[...] = x_ref[...] + mod_ref[gate_idx:gate_idx + 1, :] * y


def _mlp(h, w1, w2, x, mod3, gate_idx):
    r, d = x.shape
    d_ff = w1.shape[1]
    rows_per_group = r // mod3.shape[0]
    tm = _tile(rows_per_group, 512)
    tf = _tile(d_ff, 512)
    tn = _tile(d, 256)
    n_f = d_ff // tf

    def out_col(s):
        return jnp.maximum(s - n_f, 0)

    return pl.pallas_call(
        functools.partial(_mlp_kernel, gate_idx=gate_idx, n_f=n_f, tf=tf),
        grid=(r // tm, n_f + d // tn),
        in_specs=[pl.BlockSpec((tm, d), lambda i, s: (i, 0)),
                  pl.BlockSpec((d, tf), lambda i, s: (0, jnp.minimum(s, n_f - 1))),
                  pl.BlockSpec((d_ff, tn), lambda i, s: (0, out_col(s))),
                  pl.BlockSpec((tm, tn), lambda i, s: (i, out_col(s))),
                  pl.BlockSpec((None, _N_MOD, tn),
                               lambda i, s: ((i * tm) // rows_per_group, 0, out_col(s)))],
        out_specs=pl.BlockSpec((tm, tn), lambda i, s: (i, out_col(s))),
        out_shape=jax.ShapeDtypeStruct((r, d), _F32),
        scratch_shapes=[pltpu.VMEM((tm, d_ff), _BF16)],
        compiler_params=_params("parallel", "arbitrary"),
        name="mlp",
    )(h, w1, w2, x, mod3)


def _rope_tables(n_tokens, qk):
    axis_dim = qk // 2
    rows = n_tokens // _GRID_W
    row = jnp.repeat(jnp.arange(rows, dtype=_F32), _GRID_W)
    col = jnp.tile(jnp.arange(_GRID_W, dtype=_F32), rows)
    inv = jnp.power(_ROPE_THETA, -jnp.arange(0, axis_dim, 2, dtype=_F32) / axis_dim)
    row_ang, col_ang = row[:, None] * inv, col[:, None] * inv
    cos = jnp.concatenate([jnp.cos(row_ang)] * 2 + [jnp.cos(col_ang)] * 2, axis=-1)
    sin = jnp.concatenate([-jnp.sin(row_ang), jnp.sin(row_ang),
                           -jnp.sin(col_ang), jnp.sin(col_ang)], axis=-1)
    return cos, sin


def _trunk(x, mod3, seq, w, n_heads, qk, lam_init, rope=None, cache=None):
    d_att = n_heads * 2 * qk
    context = rope is None
    h = _prenorm(x, w["norm_attn_g"], mod3, 0, 1)
    if context:
        w = dict(w)
        qkv, new_k, new_v, w["w_out"] = _qkv_proj(h, w["w_in"], w["q_norm_g"], w["k_norm_g"], d_att, qk,
                                                  cast_src=w["w_out"])
        conv, w["w_mlp_in"] = _conv_proj(h, w["w_in"], w["conv_w"], d_att, seq, cast_src=w["w_mlp_in"])
    else:
        qkv = _qkv_proj(h, w["w_in"], w["q_norm_g"], w["k_norm_g"], d_att, qk, rope)
        conv = _conv_proj(h, w["w_in"], w["conv_w"], d_att, seq)
        new_k = new_v = None
    attn = _attention(qkv, w["lam_vecs"], w["subln_g"], n_heads, qk, seq, lam_init, cache)
    if context:
        x2, w["w_mlp_out"] = _out_proj(attn, conv, w["w_out"], x, mod3, 2, cast_src=w["w_mlp_out"])
    else:
        x2 = _out_proj(attn, conv, w["w_out"], x, mod3, 2)
    h2 = _prenorm(x2, w["norm_mlp_g"], mod3, 3, 4)
    y = _mlp(h2, w["w_mlp_in"], w["w_mlp_out"], x2, mod3, 5)
    return y, new_k, new_v, w


def kernel(x_prompt, x_sample, cache_k, cache_v, c, c_ctx, w_ada, b_ada, norm_attn_g, w_in, q_norm_g, k_norm_g, lambda_q1, lambda_k1, lambda_q2, lambda_k2, subln_g, conv_w, w_out, norm_mlp_g, w_mlp_in, w_mlp_out):
    assert w_in.shape[0] == 1, "one trunk layer"
    batch, seq, d = x_prompt.shape
    dec_batch, dec_seq, _ = x_sample.shape
    past, n_heads, _, qk = cache_k.shape[2:]
    d_att = n_heads * 2 * qk
    layer = 0
    lam_init = 0.8 - 0.6 * math.exp(-0.3 * layer)

    w = {
        "norm_attn_g": norm_attn_g[layer], "q_norm_g": q_norm_g[layer], "k_norm_g": k_norm_g[layer],
        "subln_g": subln_g[layer], "conv_w": conv_w[layer], "norm_mlp_g": norm_mlp_g[layer],
        "lam_vecs": jnp.stack([lambda_q1[layer], lambda_k1[layer], lambda_q2[layer], lambda_k2[layer]]),
        "w_in": w_in[layer].astype(_BF16), "w_out": w_out[layer],
        "w_mlp_in": w_mlp_in[layer], "w_mlp_out": w_mlp_out[layer],
    }

    n_cond = 1 + dec_batch
    cond = jnp.concatenate([c_ctx[None], c, jnp.zeros((-n_cond % 8, d), _F32)], axis=0)
    mod = _modulation(cond, w_ada[layer], b_ada[layer]).reshape(cond.shape[0], _N_MOD, d)

    y_p, new_k, new_v, w = _trunk(x_prompt.reshape(batch * seq, d), mod[0:1], seq, w,
                                  n_heads, qk, lam_init)
    cache = (cache_k[:, layer].reshape(dec_batch, past, d_att),
             cache_v[:, layer].reshape(dec_batch, past, d_att))
    y_s = _trunk(x_sample.reshape(dec_batch * dec_seq, d), mod[1:n_cond], dec_seq, w,
                 n_heads, qk, lam_init, rope=_rope_tables(dec_seq, qk), cache=cache)[0]

    return (y_p.reshape(batch, seq, d),
            y_s.reshape(dec_batch, dec_seq, d),
            new_k.reshape(batch, 1, seq, n_heads, 2, qk),
            new_v.reshape(batch, 1, seq, n_heads, 2 * qk))
```

```python
import functools
import math

import jax
import jax.numpy as jnp
from jax import lax
from jax.experimental import pallas as pl
from jax.experimental.pallas import tpu as pltpu

_GRID_W = 64
_ROPE_THETA = 10000.0
_EPS = 1e-6
_N_MOD = 6
_LANES = 128
_VMEM_LIMIT = 60 * 1024 * 1024

_F32 = jnp.float32
_BF16 = jnp.bfloat16


def _tile(dim, pref):
    t = min(dim, pref)
    assert dim % t == 0, (dim, pref)
    return t


def _params(*sem):
    return pltpu.CompilerParams(dimension_semantics=sem, vmem_limit_bytes=_VMEM_LIMIT)


class _SideCast:
    def __init__(self, src, n_i, n_j):
        rows, cols = src.shape
        n_blocks = 1
        while n_blocks * 2 <= n_i * n_j and rows % (n_blocks * 2) == 0 and rows // (n_blocks * 2) >= 16:
            n_blocks *= 2
        self.src, self.n_j, self.n_blocks, self.all_steps = src, n_j, n_blocks, n_blocks == n_i * n_j
        block = (rows // n_blocks, cols)
        self.spec = pl.BlockSpec(block, lambda i, j: (jnp.minimum(i * n_j + j, n_blocks - 1), 0))
        self.out_shape = jax.ShapeDtypeStruct(src.shape, _BF16)

    def run(self, src_ref, dst_ref):
        def cast():
            dst_ref[...] = src_ref[...].astype(dst_ref.dtype)
        if self.all_steps:
            cast()
        else:
            pl.when(pl.program_id(0) * self.n_j + pl.program_id(1) < self.n_blocks)(cast)


def _mod_kernel(cond_ref, w_ref, b_ref, o_ref):
    c = cond_ref[...]
    s = c / (1.0 + jnp.exp(-c))
    o_ref[...] = jnp.dot(s, w_ref[...], preferred_element_type=_F32) + b_ref[...]


def _modulation(cond, w_ada, b_ada):
    r, d = cond.shape
    n = w_ada.shape[1]
    tn = _tile(n, 512)
    return pl.pallas_call(
        _mod_kernel,
        grid=(n // tn,),
        in_specs=[pl.BlockSpec((r, d), lambda j: (0, 0)),
                  pl.BlockSpec((d, tn), lambda j: (0, j)),
                  pl.BlockSpec((1, tn), lambda j: (0, j))],
        out_specs=pl.BlockSpec((r, tn), lambda j: (0, j)),
        out_shape=jax.ShapeDtypeStruct((r, n), _F32),
        compiler_params=_params("arbitrary"),
        name="modulation",
    )(cond, w_ada, b_ada.reshape(1, n))


def _prenorm_kernel(x_ref, g_ref, mod_ref, o_ref, *, shift_idx, scale_idx):
    x = x_ref[...]
    y = x * lax.rsqrt(jnp.mean(x * x, axis=-1, keepdims=True) + _EPS) * g_ref[...]
    scale = mod_ref[scale_idx:scale_idx + 1, :]
    shift = mod_ref[shift_idx:shift_idx + 1, :]
    o_ref[...] = (y * (1.0 + scale) + shift).astype(o_ref.dtype)


def _prenorm(x, g, mod3, shift_idx, scale_idx):
    r, d = x.shape
    rows_per_group = r // mod3.shape[0]
    tr = _tile(rows_per_group, 256)
    return pl.pallas_call(
        functools.partial(_prenorm_kernel, shift_idx=shift_idx, scale_idx=scale_idx),
        grid=(r // tr,),
        in_specs=[pl.BlockSpec((tr, d), lambda i: (i, 0)),
                  pl.BlockSpec((1, d), lambda i: (0, 0)),
                  pl.BlockSpec((None, _N_MOD, d), lambda i: ((i * tr) // rows_per_group, 0, 0))],
        out_specs=pl.BlockSpec((tr, d), lambda i: (i, 0)),
        out_shape=jax.ShapeDtypeStruct((r, d), _BF16),
        compiler_params=_params("parallel"),
        name="prenorm",
    )(x, g.reshape(1, d), mod3)


def _headnorm(xc, g, scale):
    r = lax.rsqrt(jnp.mean(xc * xc, axis=-1, keepdims=True) + _EPS)
    if scale != 1.0:
        r = r * scale
    return xc * r * g


def _rope(y, cos, sin_signed, first_half):
    qk = y.shape[-1]
    ahead = pltpu.roll(y, qk - qk // 4, axis=1)
    behind = pltpu.roll(y, qk // 4, axis=1)
    return y * cos + jnp.where(first_half, ahead, behind) * sin_signed


def _qkv_kernel(*refs, latent, qk, n_sec, q_scale, side_cast):
    if latent:
        x_ref, w_ref, qg_ref, kg_ref, cos_ref, sin_ref, qkv_ref = refs
    else:
        x_ref, w_ref, qg_ref, kg_ref, src_ref, qkv_ref, newk_ref, newv_ref, dst_ref = refs
        side_cast.run(src_ref, dst_ref)
    j = pl.program_id(1)
    tn = w_ref.shape[1]

    def project():
        return jnp.dot(x_ref[...], w_ref[...], preferred_element_type=_F32)

    if latent:
        shared = project()
        project = lambda: shared

    def qk_epilogue(g_ref, scale, f32_ref):
        acc = project()
        g = g_ref[...]
        if latent:
            cos = cos_ref[...]
            sin = sin_ref[...]
            lane = lax.broadcasted_iota(jnp.int32, (1, qk), 1)
            first_half = (lane % (qk // 2)) < (qk // 4)
        for c in range(tn // qk):
            cols = slice(c * qk, (c + 1) * qk)
            y = _headnorm(acc[:, cols], g, scale)
            if latent:
                y = _rope(y, cos, sin, first_half)
            if f32_ref is not None:
                f32_ref[:, cols] = y
            qkv_ref[:, cols] = y.astype(qkv_ref.dtype)

    @pl.when(j < n_sec)
    def _():
        qk_epilogue(qg_ref, q_scale, None)

    @pl.when((j >= n_sec) & (j < 2 * n_sec))
    def _():
        qk_epilogue(kg_ref, 1.0, None if latent else newk_ref)

    @pl.when(j >= 2 * n_sec)
    def _():
        acc = project()
        qkv_ref[...] = acc.astype(qkv_ref.dtype)
        if not latent:
            newv_ref[...] = acc


def _qkv_proj(h, w_in, q_g, k_g, d_att, qk, rope=None, cast_src=None):
    r, d = h.shape
    latent = rope is not None
    tm = _tile(r if not latent else rope[0].shape[0], 1024)
    tn = _tile(d_att, 512)
    n_sec = d_att // tn
    in_specs = [pl.BlockSpec((tm, d), lambda i, j: (i, 0)),
                pl.BlockSpec((d, tn), lambda i, j: (0, j)),
                pl.BlockSpec((1, qk), lambda i, j: (0, 0)),
                pl.BlockSpec((1, qk), lambda i, j: (0, 0))]
    args = [h, w_in, q_g.reshape(1, qk), k_g.reshape(1, qk)]
    qkv_spec = pl.BlockSpec((tm, tn), lambda i, j: (i, j))
    qkv_shape = jax.ShapeDtypeStruct((r, 3 * d_att), _BF16)
    if latent:
        seq_tiles = rope[0].shape[0] // tm
        rope_spec = pl.BlockSpec((tm, qk), lambda i, j: (i % seq_tiles, 0))
        in_specs += [rope_spec, rope_spec]
        args += list(rope)
        out_specs, out_shape = qkv_spec, qkv_shape
        side_cast = None
    else:
        side_cast = _SideCast(cast_src, r // tm, 3 * n_sec)
        in_specs.append(side_cast.spec)
        args.append(cast_src)
        out_specs = [qkv_spec,
                     pl.BlockSpec((tm, tn), lambda i, j: (i, jnp.clip(j - n_sec, 0, n_sec - 1))),
                     pl.BlockSpec((tm, tn), lambda i, j: (i, jnp.clip(j - 2 * n_sec, 0, n_sec - 1))),
                     side_cast.spec]
        out_shape = [qkv_shape,
                     jax.ShapeDtypeStruct((r, d_att), _F32),
                     jax.ShapeDtypeStruct((r, d_att), _F32),
                     side_cast.out_shape]
    return pl.pallas_call(
        functools.partial(_qkv_kernel, latent=latent, qk=qk, n_sec=n_sec, q_scale=qk ** -0.5,
                          side_cast=side_cast),
        grid=(r // tm, 3 * n_sec),
        in_specs=in_specs, out_specs=out_specs, out_shape=out_shape,
        compiler_params=_params("parallel", "arbitrary"),
        name="qkv_proj_latent" if latent else "qkv_proj_context",
    )(*args)


def _conv_kernel(x_ref, wb_ref, wc_ref, wu_ref, cw_ref, *rest, seq, side_cast):
    if side_cast is None:
        o_ref, = rest
    else:
        src_ref, o_ref, dst_ref = rest
        side_cast.run(src_ref, dst_ref)
    x = x_ref[...]
    b = jnp.dot(x, wb_ref[...], preferred_element_type=_F32)
    c = jnp.dot(x, wc_ref[...], preferred_element_type=_F32)
    u = jnp.dot(x, wu_ref[...], preferred_element_type=_F32)
    z = c * u
    tm = z.shape[0]
    pos = lax.broadcasted_iota(jnp.int32, (tm, 1), 0) % seq
    z_prev = jnp.where(pos == 0, 0.0, pltpu.roll(z, 1, axis=0))
    z_next = jnp.where(pos == seq - 1, 0.0, pltpu.roll(z, tm - 1, axis=0))
    cw = cw_ref[...]
    y = cw[0:1, :] * z_prev + cw[1:2, :] * z + cw[2:3, :] * z_next
    o_ref[...] = (b * y).astype(o_ref.dtype)


def _conv_proj(h, w_in, conv_w, d_att, seq, cast_src=None):
    r, d = h.shape
    d_conv = conv_w.shape[1]
    tm = _tile(r, max(seq, 1024))
    assert tm % seq == 0
    tc = _tile(d_conv, 256)
    off = 3 * d_att // tc
    nc = d_conv // tc

    def w_spec(sec):
        return pl.BlockSpec((d, tc), lambda i, j: (0, off + sec * nc + j))

    in_specs = [pl.BlockSpec((tm, d), lambda i, j: (i, 0)),
                w_spec(0), w_spec(1), w_spec(2),
                pl.BlockSpec((conv_w.shape[0], tc), lambda i, j: (0, j))]
    args = [h, w_in, w_in, w_in, conv_w]
    out_specs = pl.BlockSpec((tm, tc), lambda i, j: (i, j))
    out_shape = jax.ShapeDtypeStruct((r, d_conv), _BF16)
    side_cast = None
    if cast_src is not None:
        side_cast = _SideCast(cast_src, r // tm, nc)
        in_specs.append(side_cast.spec)
        args.append(cast_src)
        out_specs, out_shape = [out_specs, side_cast.spec], [out_shape, side_cast.out_shape]
    return pl.pallas_call(
        functools.partial(_conv_kernel, seq=seq, side_cast=side_cast),
        grid=(r // tm, nc),
        in_specs=in_specs, out_specs=out_specs, out_shape=out_shape,
        compiler_params=_params("parallel", "arbitrary"),
        name="conv_proj",
    )(*args)


def _attn_kernel(*refs, n_heads, qk, lam_init, cached):
    if cached:
        q_ref, k_ref, v_ref, ck_ref, cv_ref, lam_ref, g_ref, o_ref = refs
    else:
        q_ref, k_ref, v_ref, lam_ref, g_ref, o_ref = refs
    hw = 2 * qk
    lv = lam_ref[...]
    lam = (jnp.exp(jnp.sum(lv[0:1] * lv[1:2], axis=-1, keepdims=True))
           - jnp.exp(jnp.sum(lv[2:3] * lv[3:4], axis=-1, keepdims=True)) + lam_init)
    g = g_ref[...]
    dims = (((1,), (1,)), ((), ()))

    def probs(q, keys):
        s = [lax.dot_general(q, k, dims, preferred_element_type=_F32) for k in keys]
        m = functools.reduce(jnp.maximum, [jnp.max(x, axis=-1, keepdims=True) for x in s])
        e = [jnp.exp(x - m) for x in s]
        den = functools.reduce(jnp.add, [jnp.sum(x, axis=-1, keepdims=True) for x in e])
        return e, 1.0 / den

    for h in range(n_heads):
        o = None
        c0, c1 = h * hw, h * hw + qk
        keys0 = [k_ref[:, c0:c0 + qk]]
        keys1 = [k_ref[:, c1:c1 + qk]]
        vals = [v_ref[:, c0:c0 + hw]]
        if cached:
            keys0.append(ck_ref[:, c0:c0 + qk].astype(_BF16))
            keys1.append(ck_ref[:, c1:c1 + qk].astype(_BF16))
            vals.append(cv_ref[:, c0:c0 + hw].astype(_BF16))
        e0, r0 = probs(q_ref[:, c0:c0 + qk], keys0)
        e1, r1 = probs(q_ref[:, c1:c1 + qk], keys1)
        r1 = r1 * lam
        for p0, p1, v in zip(e0, e1, vals):
            a = (p0 * r0 - p1 * r1).astype(_BF16)
            part = jnp.dot(a, v, preferred_element_type=_F32)
            o = part if o is None else o + part
        o = o * lax.rsqrt(jnp.mean(o * o, axis=-1, keepdims=True) + _EPS) * g * (1.0 - lam_init)
        o_ref[:, c0:c0 + hw] = o.astype(o_ref.dtype)


def _attention(qkv, lam_vecs, subln_g, n_heads, qk, seq, lam_init, cache=None):
    r = qkv.shape[0]
    d_att = n_heads * 2 * qk
    tq = _tile(seq, 256)
    q_tiles = seq // tq
    in_specs = [pl.BlockSpec((tq, d_att), lambda b, t: (b * q_tiles + t, 0)),
                pl.BlockSpec((seq, d_att), lambda b, t: (b, 1)),
                pl.BlockSpec((seq, d_att), lambda b, t: (b, 2))]
    args = [qkv, qkv, qkv]
    if cache is not None:
        past = cache[0].shape[1]
        c_spec = pl.BlockSpec((None, past, d_att), lambda b, t: (b, 0, 0))
        in_specs += [c_spec, c_spec]
        args += list(cache)
    in_specs += [pl.BlockSpec(lam_vecs.shape, lambda b, t: (0, 0)),
                 pl.BlockSpec((1, 2 * qk), lambda b, t: (0, 0))]
    args += [lam_vecs, subln_g.reshape(1, 2 * qk)]
    return pl.pallas_call(
        functools.partial(_attn_kernel, n_heads=n_heads, qk=qk, lam_init=lam_init,
                          cached=cache is not None),
        grid=(r // seq, q_tiles),
        in_specs=in_specs,
        out_specs=pl.BlockSpec((tq, d_att), lambda b, t: (b * q_tiles + t, 0)),
        out_shape=jax.ShapeDtypeStruct((r, d_att), _BF16),
        compiler_params=_params("parallel", "arbitrary"),
        name="diff_attention_latent" if cache is not None else "diff_attention_context",
    )(*args)


def _outproj_kernel(a_ref, c_ref, wa_ref, wc_ref, x_ref, mod_ref, *rest, gate_idx, side_cast):
    if side_cast is None:
        o_ref, = rest
    else:
        src_ref, o_ref, dst_ref = rest
        side_cast.run(src_ref, dst_ref)
    y = (jnp.dot(a_ref[...], wa_ref[...], preferred_element_type=_F32)
         + jnp.dot(c_ref[...], wc_ref[...], preferred_element_type=_F32))
    o_ref[...] = x_ref[...] + mod_ref[gate_idx:gate_idx + 1, :] * y


def _out_proj(attn, conv, w_out, x, mod3, gate_idx, cast_src=None):
    r, d = x.shape
    d_att, d_conv = attn.shape[1], conv.shape[1]
    assert d_att == d_conv
    rows_per_group = r // mod3.shape[0]
    tm = _tile(rows_per_group, 1024)
    tn = _tile(d, 512)
    in_specs = [pl.BlockSpec((tm, d_att), lambda i, j: (i, 0)),
                pl.BlockSpec((tm, d_conv), lambda i, j: (i, 0)),
                pl.BlockSpec((d_att, tn), lambda i, j: (0, j)),
                pl.BlockSpec((d_conv, tn), lambda i, j: (1, j)),
                pl.BlockSpec((tm, tn), lambda i, j: (i, j)),
                pl.BlockSpec((None, _N_MOD, tn), lambda i, j: ((i * tm) // rows_per_group, 0, j))]
    args = [attn, conv, w_out, w_out, x, mod3]
    out_specs = pl.BlockSpec((tm, tn), lambda i, j: (i, j))
    out_shape = jax.ShapeDtypeStruct((r, d), _F32)
    side_cast = None
    if cast_src is not None:
        side_cast = _SideCast(cast_src, r // tm, d // tn)
        in_specs.append(side_cast.spec)
        args.append(cast_src)
        out_specs, out_shape = [out_specs, side_cast.spec], [out_shape, side_cast.out_shape]
    return pl.pallas_call(
        functools.partial(_outproj_kernel, gate_idx=gate_idx, side_cast=side_cast),
        grid=(r // tm, d // tn),
        in_specs=in_specs, out_specs=out_specs, out_shape=out_shape,
        compiler_params=_params("parallel", "arbitrary"),
        name="out_proj",
    )(*args)


def _mlp_kernel(h_ref, w1_ref, w2_ref, x_ref, mod_ref, o_ref, hid_ref, *, gate_idx, n_f, tf):
    s = pl.program_id(1)

    @pl.when(s < n_f)
    def _():
        hid = jnp.dot(h_ref[...], w1_ref[...], preferred_element_type=_F32)
        col = pl.multiple_of(s * tf, tf)
        hid_ref[:, pl.ds(col, tf)] = jnp.square(jnp.maximum(hid, 0.0)).astype(hid_ref.dtype)

    @pl.when(s >= n_f)
    def _():
        y = jnp.dot(hid_ref[...], w2_ref[...], preferred_element_type=_F32)
        o_ref[...] = x_ref[...] + mod_ref[gate_idx:gate_idx + 1, :] * y


def _mlp(h, w1, w2, x, mod3, gate_idx):
    r, d = x.shape
    d_ff = w1.shape[1]
    rows_per_group = r // mod3.shape[0]
    tm = _tile(rows_per_group, 512)
    tf = _tile(d_ff, 1024)
    tn = _tile(d, 256)
    n_f = d_ff // tf

    def out_col(s):
        return jnp.maximum(s - n_f, 0)

    return pl.pallas_call(
        functools.partial(_mlp_kernel, gate_idx=gate_idx, n_f=n_f, tf=tf),
        grid=(r // tm, n_f + d // tn),
        in_specs=[pl.BlockSpec((tm, d), lambda i, s: (i, 0), pipeline_mode=pl.Buffered(1)),
                  pl.BlockSpec((d, tf), lambda i, s: (0, jnp.minimum(s, n_f - 1))),
                  pl.BlockSpec((d_ff, tn), lambda i, s: (0, out_col(s))),
                  pl.BlockSpec((tm, tn), lambda i, s: (i, out_col(s))),
                  pl.BlockSpec((None, _N_MOD, tn),
                               lambda i, s: ((i * tm) // rows_per_group, 0, out_col(s)))],
        out_specs=pl.BlockSpec((tm, tn), lambda i, s: (i, out_col(s))),
        out_shape=jax.ShapeDtypeStruct((r, d), _F32),
        scratch_shapes=[pltpu.VMEM((tm, d_ff), _BF16)],
        compiler_params=_params("parallel", "arbitrary"),
        name="mlp",
    )(h, w1, w2, x, mod3)


def _rope_tables(n_tokens, qk):
    axis_dim = qk // 2
    rows = n_tokens // _GRID_W
    row = jnp.repeat(jnp.arange(rows, dtype=_F32), _GRID_W)
    col = jnp.tile(jnp.arange(_GRID_W, dtype=_F32), rows)
    inv = jnp.power(_ROPE_THETA, -jnp.arange(0, axis_dim, 2, dtype=_F32) / axis_dim)
    row_ang, col_ang = row[:, None] * inv, col[:, None] * inv
    cos = jnp.concatenate([jnp.cos(row_ang)] * 2 + [jnp.cos(col_ang)] * 2, axis=-1)
    sin = jnp.concatenate([-jnp.sin(row_ang), jnp.sin(row_ang),
                           -jnp.sin(col_ang), jnp.sin(col_ang)], axis=-1)
    return cos, sin


def _trunk(x, mod3, seq, w, n_heads, qk, lam_init, rope=None, cache=None):
    d_att = n_heads * 2 * qk
    context = rope is None
    h = _prenorm(x, w["norm_attn_g"], mod3, 0, 1)
    if context:
        w = dict(w)
        qkv, new_k, new_v, w["w_out"] = _qkv_proj(h, w["w_in"], w["q_norm_g"], w["k_norm_g"], d_att, qk,
                                                  cast_src=w["w_out"])
        conv, w["w_mlp_in"] = _conv_proj(h, w["w_in"], w["conv_w"], d_att, seq, cast_src=w["w_mlp_in"])
    else:
        qkv = _qkv_proj(h, w["w_in"], w["q_norm_g"], w["k_norm_g"], d_att, qk, rope)
        conv = _conv_proj(h, w["w_in"], w["conv_w"], d_att, seq)
        new_k = new_v = None
    attn = _attention(qkv, w["lam_vecs"], w["subln_g"], n_heads, qk, seq, lam_init, cache)
    if context:
        x2, w["w_mlp_out"] = _out_proj(attn, conv, w["w_out"], x, mod3, 2, cast_src=w["w_mlp_out"])
    else:
        x2 = _out_proj(attn, conv, w["w_out"], x, mod3, 2)
    h2 = _prenorm(x2, w["norm_mlp_g"], mod3, 3, 4)
    y = _mlp(h2, w["w_mlp_in"], w["w_mlp_out"], x2, mod3, 5)
    return y, new_k, new_v, w


def kernel(x_prompt, x_sample, cache_k, cache_v, c, c_ctx, w_ada, b_ada, norm_attn_g, w_in, q_norm_g, k_norm_g, lambda_q1, lambda_k1, lambda_q2, lambda_k2, subln_g, conv_w, w_out, norm_mlp_g, w_mlp_in, w_mlp_out):
    assert w_in.shape[0] == 1, "one trunk layer"
    batch, seq, d = x_prompt.shape
    dec_batch, dec_seq, _ = x_sample.shape
    past, n_heads, _, qk = cache_k.shape[2:]
    d_att = n_heads * 2 * qk
    layer = 0
    lam_init = 0.8 - 0.6 * math.exp(-0.3 * layer)

    w = {
        "norm_attn_g": norm_attn_g[layer], "q_norm_g": q_norm_g[layer], "k_norm_g": k_norm_g[layer],
        "subln_g": subln_g[layer], "conv_w": conv_w[layer], "norm_mlp_g": norm_mlp_g[layer],
        "lam_vecs": jnp.stack([lambda_q1[layer], lambda_k1[layer], lambda_q2[layer], lambda_k2[layer]]),
        "w_in": w_in[layer].astype(_BF16), "w_out": w_out[layer],
        "w_mlp_in": w_mlp_in[layer], "w_mlp_out": w_mlp_out[layer],
    }

    n_cond = 1 + dec_batch
    cond = jnp.concatenate([c_ctx[None], c, jnp.zeros((-n_cond % 8, d), _F32)], axis=0)
    mod = _modulation(cond, w_ada[layer], b_ada[layer]).reshape(cond.shape[0], _N_MOD, d)

    y_p, new_k, new_v, w = _trunk(x_prompt.reshape(batch * seq, d), mod[0:1], seq, w,
                                  n_heads, qk, lam_init)
    cache = (cache_k[:, layer].reshape(dec_batch, past, d_att),
             cache_v[:, layer].reshape(dec_batch, past, d_att))
    y_s = _trunk(x_sample.reshape(dec_batch * dec_seq, d), mod[1:n_cond], dec_seq, w,
                 n_heads, qk, lam_init, rope=_rope_tables(dec_seq, qk), cache=cache)[0]

    return (y_p.reshape(batch, seq, d),
            y_s.reshape(dec_batch, dec_seq, d),
            new_k.reshape(batch, 1, seq, n_heads, 2, qk),
            new_v.reshape(batch, 1, seq, n_heads, 2 * qk))
```

```python
import functools
import math

import jax
import jax.numpy as jnp
from jax import lax
from jax.experimental import pallas as pl
from jax.experimental.pallas import tpu as pltpu

_GRID_W = 64
_ROPE_THETA = 10000.0
_EPS = 1e-6
_N_MOD = 6
_LANES = 128
_VMEM_LIMIT = 60 * 1024 * 1024

_F32 = jnp.float32
_BF16 = jnp.bfloat16


def _tile(dim, pref):
    t = min(dim, pref)
    assert dim % t == 0, (dim, pref)
    return t


def _params(*sem):
    return pltpu.CompilerParams(dimension_semantics=sem, vmem_limit_bytes=_VMEM_LIMIT)


class _SideCast:
    def __init__(self, src, n_i, n_j):
        rows, cols = src.shape
        n_blocks = 1
        while n_blocks * 2 <= n_i * n_j and rows % (n_blocks * 2) == 0 and rows // (n_blocks * 2) >= 16:
            n_blocks *= 2
        self.src, self.n_j, self.n_blocks, self.all_steps = src, n_j, n_blocks, n_blocks == n_i * n_j
        block = (rows // n_blocks, cols)
        self.spec = pl.BlockSpec(block, lambda i, j: (jnp.minimum(i * n_j + j, n_blocks - 1), 0))
        self.out_shape = jax.ShapeDtypeStruct(src.shape, _BF16)

    def run(self, src_ref, dst_ref):
        def cast():
            dst_ref[...] = src_ref[...].astype(dst_ref.dtype)
        if self.all_steps:
            cast()
        else:
            pl.when(pl.program_id(0) * self.n_j + pl.program_id(1) < self.n_blocks)(cast)


def _mod_kernel(cond_ref, w_ref, b_ref, o_ref):
    c = cond_ref[...]
    s = c / (1.0 + jnp.exp(-c))
    o_ref[...] = jnp.dot(s, w_ref[...], preferred_element_type=_F32) + b_ref[...]


def _modulation(cond, w_ada, b_ada):
    r, d = cond.shape
    n = w_ada.shape[1]
    tn = _tile(n, 512)
    return pl.pallas_call(
        _mod_kernel,
        grid=(n // tn,),
        in_specs=[pl.BlockSpec((r, d), lambda j: (0, 0)),
                  pl.BlockSpec((d, tn), lambda j: (0, j)),
                  pl.BlockSpec((1, tn), lambda j: (0, j))],
        out_specs=pl.BlockSpec((r, tn), lambda j: (0, j)),
        out_shape=jax.ShapeDtypeStruct((r, n), _F32),
        compiler_params=_params("arbitrary"),
        name="modulation",
    )(cond, w_ada, b_ada.reshape(1, n))


def _prenorm_kernel(x_ref, g_ref, mod_ref, o_ref, *, shift_idx, scale_idx):
    x = x_ref[...]
    y = x * lax.rsqrt(jnp.mean(x * x, axis=-1, keepdims=True) + _EPS) * g_ref[...]
    scale = mod_ref[scale_idx:scale_idx + 1, :]
    shift = mod_ref[shift_idx:shift_idx + 1, :]
    o_ref[...] = (y * (1.0 + scale) + shift).astype(o_ref.dtype)


def _prenorm(x, g, mod3, shift_idx, scale_idx):
    r, d = x.shape
    rows_per_group = r // mod3.shape[0]
    tr = _tile(rows_per_group, 512)
    return pl.pallas_call(
        functools.partial(_prenorm_kernel, shift_idx=shift_idx, scale_idx=scale_idx),
        grid=(r // tr,),
        in_specs=[pl.BlockSpec((tr, d), lambda i: (i, 0)),
                  pl.BlockSpec((1, d), lambda i: (0, 0)),
                  pl.BlockSpec((None, _N_MOD, d), lambda i: ((i * tr) // rows_per_group, 0, 0))],
        out_specs=pl.BlockSpec((tr, d), lambda i: (i, 0)),
        out_shape=jax.ShapeDtypeStruct((r, d), _BF16),
        compiler_params=_params("parallel"),
        name="prenorm",
    )(x, g.reshape(1, d), mod3)


def _headnorm(xc, g, scale):
    r = lax.rsqrt(jnp.mean(xc * xc, axis=-1, keepdims=True) + _EPS)
    if scale != 1.0:
        r = r * scale
    return xc * r * g


def _rope(y, cos, sin_signed, first_half):
    qk = y.shape[-1]
    ahead = pltpu.roll(y, qk - qk // 4, axis=1)
    behind = pltpu.roll(y, qk // 4, axis=1)
    return y * cos + jnp.where(first_half, ahead, behind) * sin_signed


def _qkv_kernel(*refs, latent, qk, n_sec, q_scale, side_cast):
    if latent:
        x_ref, w_ref, qg_ref, kg_ref, cos_ref, sin_ref, qkv_ref = refs
    else:
        x_ref, w_ref, qg_ref, kg_ref, src_ref, qkv_ref, newk_ref, newv_ref, dst_ref = refs
        side_cast.run(src_ref, dst_ref)
    j = pl.program_id(1)
    tn = w_ref.shape[1]

    def project():
        return jnp.dot(x_ref[...], w_ref[...], preferred_element_type=_F32)

    if latent:
        shared = project()
        project = lambda: shared

    def qk_epilogue(g_ref, scale, f32_ref):
        acc = project()
        g = g_ref[...]
        if latent:
            cos = cos_ref[...]
            sin = sin_ref[...]
            lane = lax.broadcasted_iota(jnp.int32, (1, qk), 1)
            first_half = (lane % (qk // 2)) < (qk // 4)
        for c in range(tn // qk):
            cols = slice(c * qk, (c + 1) * qk)
            y = _headnorm(acc[:, cols], g, scale)
            if latent:
                y = _rope(y, cos, sin, first_half)
            if f32_ref is not None:
                n_chunks = n_sec * (tn // qk)
                chunk = (j - n_sec) * (tn // qk) + c
                f32_ref[pl.ds(chunk, acc.shape[0], stride=n_chunks), :] = y
            qkv_ref[:, cols] = y.astype(qkv_ref.dtype)

    @pl.when(j < n_sec)
    def _():
        qk_epilogue(qg_ref, q_scale, None)

    @pl.when((j >= n_sec) & (j < 2 * n_sec))
    def _():
        qk_epilogue(kg_ref, 1.0, None if latent else newk_ref)

    @pl.when(j >= 2 * n_sec)
    def _():
        acc = project()
        qkv_ref[...] = acc.astype(qkv_ref.dtype)
        if not latent:
            newv_ref[...] = acc


def _qkv_proj(h, w_in, q_g, k_g, d_att, qk, rope=None, cast_src=None):
    r, d = h.shape
    latent = rope is not None
    tm = _tile(r if not latent else rope[0].shape[0], 1024)
    tn = _tile(d_att, 512)
    n_sec = d_att // tn
    in_specs = [pl.BlockSpec((tm, d), lambda i, j: (i, 0)),
                pl.BlockSpec((d, tn), lambda i, j: (0, j)),
                pl.BlockSpec((1, qk), lambda i, j: (0, 0)),
                pl.BlockSpec((1, qk), lambda i, j: (0, 0))]
    args = [h, w_in, q_g.reshape(1, qk), k_g.reshape(1, qk)]
    qkv_spec = pl.BlockSpec((tm, tn), lambda i, j: (i, j))
    qkv_shape = jax.ShapeDtypeStruct((r, 3 * d_att), _BF16)
    if latent:
        seq_tiles = rope[0].shape[0] // tm
        rope_spec = pl.BlockSpec((tm, qk), lambda i, j: (i % seq_tiles, 0))
        in_specs += [rope_spec, rope_spec]
        args += list(rope)
        out_specs, out_shape = qkv_spec, qkv_shape
        side_cast = None
    else:
        side_cast = _SideCast(cast_src, r // tm, 3 * n_sec)
        in_specs.append(side_cast.spec)
        args.append(cast_src)
        n_chunks = d_att // qk
        out_specs = [qkv_spec,
                     pl.BlockSpec((tm * n_chunks, qk), lambda i, j: (i, 0)),
                     pl.BlockSpec((tm, tn), lambda i, j: (i, jnp.clip(j - 2 * n_sec, 0, n_sec - 1))),
                     side_cast.spec]
        out_shape = [qkv_shape,
                     jax.ShapeDtypeStruct((r * n_chunks, qk), _F32),
                     jax.ShapeDtypeStruct((r, d_att), _F32),
                     side_cast.out_shape]
    return pl.pallas_call(
        functools.partial(_qkv_kernel, latent=latent, qk=qk, n_sec=n_sec, q_scale=qk ** -0.5,
                          side_cast=side_cast),
        grid=(r // tm, 3 * n_sec),
        in_specs=in_specs, out_specs=out_specs, out_shape=out_shape,
        compiler_params=_params("parallel", "arbitrary"),
        name="qkv_proj_latent" if latent else "qkv_proj_context",
    )(*args)


def _conv_kernel(x_ref, wb_ref, wc_ref, wu_ref, cw_ref, *rest, seq, side_cast):
    if side_cast is None:
        o_ref, = rest
    else:
        src_ref, o_ref, dst_ref = rest
        side_cast.run(src_ref, dst_ref)
    x = x_ref[...]
    b = jnp.dot(x, wb_ref[...], preferred_element_type=_F32)
    c = jnp.dot(x, wc_ref[...], preferred_element_type=_F32)
    u = jnp.dot(x, wu_ref[...], preferred_element_type=_F32)
    z = c * u
    tm = z.shape[0]
    pos = lax.broadcasted_iota(jnp.int32, (tm, 1), 0) % seq
    z_prev = jnp.where(pos == 0, 0.0, pltpu.roll(z, 1, axis=0))
    z_next = jnp.where(pos == seq - 1, 0.0, pltpu.roll(z, tm - 1, axis=0))
    cw = cw_ref[...]
    y = cw[0:1, :] * z_prev + cw[1:2, :] * z + cw[2:3, :] * z_next
    o_ref[...] = (b * y).astype(o_ref.dtype)


def _conv_proj(h, w_in, conv_w, d_att, seq, cast_src=None):
    r, d = h.shape
    d_conv = conv_w.shape[1]
    tm = _tile(r, max(seq, 1024))
    assert tm % seq == 0
    tc = _tile(d_conv, 256)
    off = 3 * d_att // tc
    nc = d_conv // tc

    def w_spec(sec):
        return pl.BlockSpec((d, tc), lambda i, j: (0, off + sec * nc + j))

    in_specs = [pl.BlockSpec((tm, d), lambda i, j: (i, 0)),
                w_spec(0), w_spec(1), w_spec(2),
                pl.BlockSpec((conv_w.shape[0], tc), lambda i, j: (0, j))]
    args = [h, w_in, w_in, w_in, conv_w]
    out_specs = pl.BlockSpec((tm, tc), lambda i, j: (i, j))
    out_shape = jax.ShapeDtypeStruct((r, d_conv), _BF16)
    side_cast = None
    if cast_src is not None:
        side_cast = _SideCast(cast_src, r // tm, nc)
        in_specs.append(side_cast.spec)
        args.append(cast_src)
        out_specs, out_shape = [out_specs, side_cast.spec], [out_shape, side_cast.out_shape]
    return pl.pallas_call(
        functools.partial(_conv_kernel, seq=seq, side_cast=side_cast),
        grid=(r // tm, nc),
        in_specs=in_specs, out_specs=out_specs, out_shape=out_shape,
        compiler_params=_params("parallel", "arbitrary"),
        name="conv_proj",
    )(*args)


def _attn_kernel(*refs, n_heads, qk, lam_init, cached):
    hw = 2 * qk
    if cached:
        q_ref, k_ref, v_ref, ck_ref, cv_ref, lam_ref, g_ref, o_ref, ckb_ref, cvb_ref = refs
        past = cv_ref.shape[0]

        @pl.when(pl.program_id(1) == 0)
        def _():
            for h in range(n_heads):
                for c in range(2):
                    rows = pl.ds(2 * h + c, past, stride=2 * n_heads)
                    ckb_ref[:, h * hw + c * qk:h * hw + (c + 1) * qk] = ck_ref[rows, :].astype(_BF16)
                cvb_ref[:, h * hw:(h + 1) * hw] = cv_ref[:, h, :].astype(_BF16)
    else:
        q_ref, k_ref, v_ref, lam_ref, g_ref, o_ref = refs
    lv = lam_ref[...]
    lam = (jnp.exp(jnp.sum(lv[0:1] * lv[1:2], axis=-1, keepdims=True))
           - jnp.exp(jnp.sum(lv[2:3] * lv[3:4], axis=-1, keepdims=True)) + lam_init)
    g = g_ref[...]
    dims = (((1,), (1,)), ((), ()))

    def probs(q, keys):
        s = [lax.dot_general(q, k, dims, preferred_element_type=_F32) for k in keys]
        m = functools.reduce(jnp.maximum, [jnp.max(x, axis=-1, keepdims=True) for x in s])
        e = [jnp.exp(x - m) for x in s]
        den = functools.reduce(jnp.add, [jnp.sum(x, axis=-1, keepdims=True) for x in e])
        return e, 1.0 / den

    for h in range(n_heads):
        o = None
        c0, c1 = h * hw, h * hw + qk
        keys0 = [k_ref[:, c0:c0 + qk]]
        keys1 = [k_ref[:, c1:c1 + qk]]
        vals = [v_ref[:, c0:c0 + hw]]
        if cached:
            keys0.append(ckb_ref[:, c0:c0 + qk])
            keys1.append(ckb_ref[:, c1:c1 + qk])
            vals.append(cvb_ref[:, c0:c0 + hw])
        e0, r0 = probs(q_ref[:, c0:c0 + qk], keys0)
        e1, r1 = probs(q_ref[:, c1:c1 + qk], keys1)
        r1 = r1 * lam
        for p0, p1, v in zip(e0, e1, vals):
            a = (p0 * r0 - p1 * r1).astype(_BF16)
            part = jnp.dot(a, v, preferred_element_type=_F32)
            o = part if o is None else o + part
        o = o * lax.rsqrt(jnp.mean(o * o, axis=-1, keepdims=True) + _EPS) * g * (1.0 - lam_init)
        o_ref[:, c0:c0 + hw] = o.astype(o_ref.dtype)


def _attention(qkv, lam_vecs, subln_g, n_heads, qk, seq, lam_init, cache=None):
    r = qkv.shape[0]
    d_att = n_heads * 2 * qk
    tq = _tile(seq, 256)
    q_tiles = seq // tq
    in_specs = [pl.BlockSpec((tq, d_att), lambda b, t: (b * q_tiles + t, 0)),
                pl.BlockSpec((seq, d_att), lambda b, t: (b, 1)),
                pl.BlockSpec((seq, d_att), lambda b, t: (b, 2))]
    args = [qkv, qkv, qkv]
    if cache is not None:
        ck, cv = cache
        in_specs += [pl.BlockSpec((None,) + ck.shape[1:], lambda b, t: (b, 0, 0)),
                     pl.BlockSpec((None,) + cv.shape[1:], lambda b, t: (b, 0, 0, 0))]
        args += [ck, cv]
        scratch = [pltpu.VMEM((cv.shape[1], d_att), _BF16)] * 2
    else:
        scratch = []
    in_specs += [pl.BlockSpec(lam_vecs.shape, lambda b, t: (0, 0)),
                 pl.BlockSpec((1, 2 * qk), lambda b, t: (0, 0))]
    args += [lam_vecs, subln_g.reshape(1, 2 * qk)]
    return pl.pallas_call(
        functools.partial(_attn_kernel, n_heads=n_heads, qk=qk, lam_init=lam_init,
                          cached=cache is not None),
        grid=(r // seq, q_tiles),
        in_specs=in_specs,
        out_specs=pl.BlockSpec((tq, d_att), lambda b, t: (b * q_tiles + t, 0)),
        out_shape=jax.ShapeDtypeStruct((r, d_att), _BF16),
        scratch_shapes=scratch,
        compiler_params=_params("parallel", "arbitrary"),
        name="diff_attention_latent" if cache is not None else "diff_attention_context",
    )(*args)


def _outproj_kernel(a_ref, c_ref, wa_ref, wc_ref, x_ref, mod_ref, *rest, gate_idx, side_cast):
    if side_cast is None:
        o_ref, = rest
    else:
        src_ref, o_ref, dst_ref = rest
        side_cast.run(src_ref, dst_ref)
    y = (jnp.dot(a_ref[...], wa_ref[...], preferred_element_type=_F32)
         + jnp.dot(c_ref[...], wc_ref[...], preferred_element_type=_F32))
    o_ref[...] = x_ref[...] + mod_ref[gate_idx:gate_idx + 1, :] * y


def _out_proj(attn, conv, w_out, x, mod3, gate_idx, cast_src=None):
    r, d = x.shape
    d_att, d_conv = attn.shape[1], conv.shape[1]
    assert d_att == d_conv
    rows_per_group = r // mod3.shape[0]
    tm = _tile(rows_per_group, 1024)
    tn = _tile(d, 512)
    in_specs = [pl.BlockSpec((tm, d_att), lambda i, j: (i, 0)),
                pl.BlockSpec((tm, d_conv), lambda i, j: (i, 0)),
                pl.BlockSpec((d_att, tn), lambda i, j: (0, j)),
                pl.BlockSpec((d_conv, tn), lambda i, j: (1, j)),
                pl.BlockSpec((tm, tn), lambda i, j: (i, j)),
                pl.BlockSpec((None, _N_MOD, tn), lambda i, j: ((i * tm) // rows_per_group, 0, j))]
    args = [attn, conv, w_out, w_out, x, mod3]
    out_specs = pl.BlockSpec((tm, tn), lambda i, j: (i, j))
    out_shape = jax.ShapeDtypeStruct((r, d), _F32)
    side_cast = None
    if cast_src is not None:
        side_cast = _SideCast(cast_src, r // tm, d // tn)
        in_specs.append(side_cast.spec)
        args.append(cast_src)
        out_specs, out_shape = [out_specs, side_cast.spec], [out_shape, side_cast.out_shape]
    return pl.pallas_call(
        functools.partial(_outproj_kernel, gate_idx=gate_idx, side_cast=side_cast),
        grid=(r // tm, d // tn),
        in_specs=in_specs, out_specs=out_specs, out_shape=out_shape,
        compiler_params=_params("parallel", "arbitrary"),
        name="out_proj",
    )(*args)


def _mlp_kernel(h_ref, w1_ref, w2_ref, x_ref, mod_ref, o_ref, hid_ref, *, gate_idx, n_f, tf):
    s = pl.program_id(1)

    @pl.when(s < n_f)
    def _():
        hid = jnp.dot(h_ref[...], w1_ref[...], preferred_element_type=_F32)
        col = pl.multiple_of(s * tf, tf)
        hid_ref[:, pl.ds(col, tf)] = jnp.square(jnp.maximum(hid, 0.0)).astype(hid_ref.dtype)

    @pl.when(s >= n_f)
    def _():
        y = jnp.dot(hid_ref[...], w2_ref[...], preferred_element_type=_F32)
        o_ref[...] = x_ref[...] + mod_ref[gate_idx:gate_idx + 1, :] * y


def _mlp(h, w1, w2, x, mod3, gate_idx):
    r, d = x.shape
    d_ff = w1.shape[1]
    rows_per_group = r // mod3.shape[0]
    tm = _tile(rows_per_group, 512)
    tf = _tile(d_ff, 1024)
    tn = _tile(d, 256)
    n_f = d_ff // tf

    def out_col(s):
        return jnp.maximum(s - n_f, 0)

    return pl.pallas_call(
        functools.partial(_mlp_kernel, gate_idx=gate_idx, n_f=n_f, tf=tf),
        grid=(r // tm, n_f + d // tn),
        in_specs=[pl.BlockSpec((tm, d), lambda i, s: (i, 0), pipeline_mode=pl.Buffered(1)),
                  pl.BlockSpec((d, tf), lambda i, s: (0, jnp.minimum(s, n_f - 1))),
                  pl.BlockSpec((d_ff, tn), lambda i, s: (0, out_col(s))),
                  pl.BlockSpec((tm, tn), lambda i, s: (i, out_col(s))),
                  pl.BlockSpec((None, _N_MOD, tn),
                               lambda i, s: ((i * tm) // rows_per_group, 0, out_col(s)))],
        out_specs=pl.BlockSpec((tm, tn), lambda i, s: (i, out_col(s))),
        out_shape=jax.ShapeDtypeStruct((r, d), _F32),
        scratch_shapes=[pltpu.VMEM((tm, d_ff), _BF16)],
        compiler_params=_params("parallel", "arbitrary"),
        name="mlp",
    )(h, w1, w2, x, mod3)


def _rope_tables(n_tokens, qk):
    axis_dim = qk // 2
    rows = n_tokens // _GRID_W
    row = jnp.repeat(jnp.arange(rows, dtype=_F32), _GRID_W)
    col = jnp.tile(jnp.arange(_GRID_W, dtype=_F32), rows)
    inv = jnp.power(_ROPE_THETA, -jnp.arange(0, axis_dim, 2, dtype=_F32) / axis_dim)
    row_ang, col_ang = row[:, None] * inv, col[:, None] * inv
    cos = jnp.concatenate([jnp.cos(row_ang)] * 2 + [jnp.cos(col_ang)] * 2, axis=-1)
    sin = jnp.concatenate([-jnp.sin(row_ang), jnp.sin(row_ang),
                           -jnp.sin(col_ang), jnp.sin(col_ang)], axis=-1)
    return cos, sin


def _trunk(x, mod3, seq, w, n_heads, qk, lam_init, rope=None, cache=None):
    d_att = n_heads * 2 * qk
    context = rope is None
    h = _prenorm(x, w["norm_attn_g"], mod3, 0, 1)
    if context:
        w = dict(w)
        qkv, new_k, new_v, w["w_out"] = _qkv_proj(h, w["w_in"], w["q_norm_g"], w["k_norm_g"], d_att, qk,
                                                  cast_src=w["w_out"])
        conv, w["w_mlp_in"] = _conv_proj(h, w["w_in"], w["conv_w"], d_att, seq, cast_src=w["w_mlp_in"])
    else:
        qkv = _qkv_proj(h, w["w_in"], w["q_norm_g"], w["k_norm_g"], d_att, qk, rope)
        conv = _conv_proj(h, w["w_in"], w["conv_w"], d_att, seq)
        new_k = new_v = None
    attn = _attention(qkv, w["lam_vecs"], w["subln_g"], n_heads, qk, seq, lam_init, cache)
    if context:
        x2, w["w_mlp_out"] = _out_proj(attn, conv, w["w_out"], x, mod3, 2, cast_src=w["w_mlp_out"])
    else:
        x2 = _out_proj(attn, conv, w["w_out"], x, mod3, 2)
    h2 = _prenorm(x2, w["norm_mlp_g"], mod3, 3, 4)
    y = _mlp(h2, w["w_mlp_in"], w["w_mlp_out"], x2, mod3, 5)
    return y, new_k, new_v, w


def kernel(x_prompt, x_sample, cache_k, cache_v, c, c_ctx, w_ada, b_ada, norm_attn_g, w_in, q_norm_g, k_norm_g, lambda_q1, lambda_k1, lambda_q2, lambda_k2, subln_g, conv_w, w_out, norm_mlp_g, w_mlp_in, w_mlp_out):
    assert w_in.shape[0] == 1, "one trunk layer"
    batch, seq, d = x_prompt.shape
    dec_batch, dec_seq, _ = x_sample.shape
    past, n_heads, _, qk = cache_k.shape[2:]
    d_att = n_heads * 2 * qk
    layer = 0
    lam_init = 0.8 - 0.6 * math.exp(-0.3 * layer)

    w = {
        "norm_attn_g": norm_attn_g[layer], "q_norm_g": q_norm_g[layer], "k_norm_g": k_norm_g[layer],
        "subln_g": subln_g[layer], "conv_w": conv_w[layer], "norm_mlp_g": norm_mlp_g[layer],
        "lam_vecs": jnp.stack([lambda_q1[layer], lambda_k1[layer], lambda_q2[layer], lambda_k2[layer]]),
        "w_in": w_in[layer].astype(_BF16), "w_out": w_out[layer],
        "w_mlp_in": w_mlp_in[layer], "w_mlp_out": w_mlp_out[layer],
    }

    n_cond = 1 + dec_batch
    cond = jnp.concatenate([c_ctx[None], c, jnp.zeros((-n_cond % 8, d), _F32)], axis=0)
    mod = _modulation(cond, w_ada[layer], b_ada[layer]).reshape(cond.shape[0], _N_MOD, d)

    y_p, new_k, new_v, w = _trunk(x_prompt.reshape(batch * seq, d), mod[0:1], seq, w,
                                  n_heads, qk, lam_init)
    cache = (cache_k[:, layer].reshape(dec_batch, past * n_heads * 2, qk), cache_v[:, layer])
    y_s = _trunk(x_sample.reshape(dec_batch * dec_seq, d), mod[1:n_cond], dec_seq, w,
                 n_heads, qk, lam_init, rope=_rope_tables(dec_seq, qk), cache=cache)[0]

    return (y_p.reshape(batch, seq, d),
            y_s.reshape(dec_batch, dec_seq, d),
            new_k.reshape(batch, 1, seq, n_heads, 2, qk),
            new_v.reshape(batch, 1, seq, n_heads, 2 * qk))
```

```python
import functools
import math

import jax
import jax.numpy as jnp
from jax import lax
from jax.experimental import pallas as pl
from jax.experimental.pallas import tpu as pltpu

_GRID_W = 64
_ROPE_THETA = 10000.0
_EPS = 1e-6
_N_MOD = 6
_LANES = 128
_VMEM_LIMIT = 60 * 1024 * 1024

_F32 = jnp.float32
_BF16 = jnp.bfloat16


def _tile(dim, pref):
    t = min(dim, pref)
    assert dim % t == 0, (dim, pref)
    return t


def _params(*sem):
    return pltpu.CompilerParams(dimension_semantics=sem, vmem_limit_bytes=_VMEM_LIMIT)


class _SideCast:
    def __init__(self, src, n_i, n_j, col_block=0, n_col_blocks=1):
        rows, cols = src.shape[0], src.shape[1] // n_col_blocks
        n_blocks = 1
        while n_blocks * 2 <= n_i * n_j and rows % (n_blocks * 2) == 0 and rows // (n_blocks * 2) >= 16:
            n_blocks *= 2
        self.src, self.n_j, self.n_blocks, self.all_steps = src, n_j, n_blocks, n_blocks == n_i * n_j
        block = (rows // n_blocks, cols)

        def row_block(*idx):
            step = idx[0] * n_j + idx[1] if len(idx) > 1 else idx[0]
            return jnp.minimum(step, n_blocks - 1)

        self.spec = pl.BlockSpec(block, lambda *idx: (row_block(*idx), col_block))
        self.out_spec = pl.BlockSpec(block, lambda *idx: (row_block(*idx), 0))
        self.out_shape = jax.ShapeDtypeStruct((rows, cols), _BF16)

    def run(self, src_ref, dst_ref):
        def cast():
            dst_ref[...] = src_ref[...].astype(dst_ref.dtype)
        if self.all_steps:
            cast()
        else:
            pl.when(pl.program_id(0) * self.n_j + pl.program_id(1) < self.n_blocks)(cast)


def _mod_kernel(cond_ref, w_ref, b_ref, o_ref):
    c = cond_ref[...]
    s = c / (1.0 + jnp.exp(-c))
    o_ref[...] = jnp.dot(s, w_ref[...], preferred_element_type=_F32) + b_ref[...]


def _modulation(cond, w_ada, b_ada):
    r, d = cond.shape
    n = w_ada.shape[1]
    tn = _tile(n, 512)
    return pl.pallas_call(
        _mod_kernel,
        grid=(n // tn,),
        in_specs=[pl.BlockSpec((r, d), lambda j: (0, 0)),
                  pl.BlockSpec((d, tn), lambda j: (0, j)),
                  pl.BlockSpec((1, tn), lambda j: (0, j))],
        out_specs=pl.BlockSpec((r, tn), lambda j: (0, j)),
        out_shape=jax.ShapeDtypeStruct((r, n), _F32),
        compiler_params=_params("arbitrary"),
        name="modulation",
    )(cond, w_ada, b_ada.reshape(1, n))


def _prenorm_kernel(x_ref, g_ref, mod_ref, o_ref, *, shift_idx, scale_idx):
    x = x_ref[...]
    y = x * lax.rsqrt(jnp.mean(x * x, axis=-1, keepdims=True) + _EPS) * g_ref[...]
    scale = mod_ref[scale_idx:scale_idx + 1, :]
    shift = mod_ref[shift_idx:shift_idx + 1, :]
    o_ref[...] = (y * (1.0 + scale) + shift).astype(o_ref.dtype)


def _prenorm(x, g, mod3, shift_idx, scale_idx):
    r, d = x.shape
    rows_per_group = r // mod3.shape[0]
    tr = _tile(rows_per_group, 512)
    return pl.pallas_call(
        functools.partial(_prenorm_kernel, shift_idx=shift_idx, scale_idx=scale_idx),
        grid=(r // tr,),
        in_specs=[pl.BlockSpec((tr, d), lambda i: (i, 0)),
                  pl.BlockSpec((1, d), lambda i: (0, 0)),
                  pl.BlockSpec((None, _N_MOD, d), lambda i: ((i * tr) // rows_per_group, 0, 0))],
        out_specs=pl.BlockSpec((tr, d), lambda i: (i, 0)),
        out_shape=jax.ShapeDtypeStruct((r, d), _BF16),
        compiler_params=_params("parallel"),
        name="prenorm",
    )(x, g.reshape(1, d), mod3)


def _headnorm(xc, g, scale):
    r = lax.rsqrt(jnp.mean(xc * xc, axis=-1, keepdims=True) + _EPS)
    if scale != 1.0:
        r = r * scale
    return xc * r * g


def _rope(y, cos, sin_signed, first_half):
    qk = y.shape[-1]
    ahead = pltpu.roll(y, qk - qk // 4, axis=1)
    behind = pltpu.roll(y, qk // 4, axis=1)
    return y * cos + jnp.where(first_half, ahead, behind) * sin_signed


def _qkv_kernel(*refs, latent, qk, n_sec, q_scale, side_cast):
    if latent:
        x_ref, w_ref, qg_ref, kg_ref, cos_ref, sin_ref, qkv_ref = refs
    else:
        x_ref, w_ref, qg_ref, kg_ref, src_ref, qkv_ref, newk_ref, newv_ref, dst_ref = refs
        side_cast.run(src_ref, dst_ref)
    j = pl.program_id(1)
    tn = w_ref.shape[1]

    def project():
        return jnp.dot(x_ref[...], w_ref[...], preferred_element_type=_F32)

    if latent:
        shared = project()
        project = lambda: shared

    def qk_epilogue(g_ref, scale, f32_ref):
        acc = project()
        g = g_ref[...]
        if latent:
            cos = cos_ref[...]
            sin = sin_ref[...]
            lane = lax.broadcasted_iota(jnp.int32, (1, qk), 1)
            first_half = (lane % (qk // 2)) < (qk // 4)
        for c in range(tn // qk):
            cols = slice(c * qk, (c + 1) * qk)
            y = _headnorm(acc[:, cols], g, scale)
            if latent:
                y = _rope(y, cos, sin, first_half)
            if f32_ref is not None:
                n_chunks = n_sec * (tn // qk)
                chunk = (j - n_sec) * (tn // qk) + c
                f32_ref[pl.ds(chunk, acc.shape[0], stride=n_chunks), :] = y
            qkv_ref[:, cols] = y.astype(qkv_ref.dtype)

    @pl.when(j < n_sec)
    def _():
        qk_epilogue(qg_ref, q_scale, None)

    @pl.when((j >= n_sec) & (j < 2 * n_sec))
    def _():
        qk_epilogue(kg_ref, 1.0, None if latent else newk_ref)

    @pl.when(j >= 2 * n_sec)
    def _():
        acc = project()
        qkv_ref[...] = acc.astype(qkv_ref.dtype)
        if not latent:
            newv_ref[...] = acc


def _qkv_proj(h, w_qkv, q_g, k_g, d_att, qk, rope=None, side_cast_args=None):
    r, d = h.shape
    latent = rope is not None
    tm = _tile(r if not latent else rope[0].shape[0], 1024)
    tn = _tile(d_att, 512)
    n_sec = d_att // tn
    in_specs = [pl.BlockSpec((tm, d), lambda i, j: (i, 0)),
                pl.BlockSpec((d, tn), lambda i, j: (0, j)),
                pl.BlockSpec((1, qk), lambda i, j: (0, 0)),
                pl.BlockSpec((1, qk), lambda i, j: (0, 0))]
    args = [h, w_qkv, q_g.reshape(1, qk), k_g.reshape(1, qk)]
    qkv_spec = pl.BlockSpec((tm, tn), lambda i, j: (i, j))
    qkv_shape = jax.ShapeDtypeStruct((r, 3 * d_att), _BF16)
    if latent:
        seq_tiles = rope[0].shape[0] // tm
        rope_spec = pl.BlockSpec((tm, qk), lambda i, j: (i % seq_tiles, 0))
        in_specs += [rope_spec, rope_spec]
        args += list(rope)
        out_specs, out_shape = qkv_spec, qkv_shape
        side_cast = None
    else:
        side_cast = _SideCast(side_cast_args[0], r // tm, 3 * n_sec, *side_cast_args[1:])
        in_specs.append(side_cast.spec)
        args.append(side_cast.src)
        n_chunks = d_att // qk
        out_specs = [qkv_spec,
                     pl.BlockSpec((tm * n_chunks, qk), lambda i, j: (i, 0)),
                     pl.BlockSpec((tm, tn), lambda i, j: (i, jnp.clip(j - 2 * n_sec, 0, n_sec - 1))),
                     side_cast.out_spec]
        out_shape = [qkv_shape,
                     jax.ShapeDtypeStruct((r * n_chunks, qk), _F32),
                     jax.ShapeDtypeStruct((r, d_att), _F32),
                     side_cast.out_shape]
    return pl.pallas_call(
        functools.partial(_qkv_kernel, latent=latent, qk=qk, n_sec=n_sec, q_scale=qk ** -0.5,
                          side_cast=side_cast),
        grid=(r // tm, 3 * n_sec),
        in_specs=in_specs, out_specs=out_specs, out_shape=out_shape,
        compiler_params=_params("parallel", "arbitrary"),
        name="qkv_proj_latent" if latent else "qkv_proj_context",
    )(*args)


def _conv_kernel(x_ref, wb_ref, wc_ref, wu_ref, cw_ref, *rest, seq, side_cast):
    if side_cast is None:
        o_ref, = rest
    else:
        src_ref, o_ref, dst_ref = rest
        side_cast.run(src_ref, dst_ref)
    x = x_ref[...]
    b = jnp.dot(x, wb_ref[...], preferred_element_type=_F32)
    c = jnp.dot(x, wc_ref[...], preferred_element_type=_F32)
    u = jnp.dot(x, wu_ref[...], preferred_element_type=_F32)
    z = c * u
    tm = z.shape[0]
    pos = lax.broadcasted_iota(jnp.int32, (tm, 1), 0) % seq
    z_prev = jnp.where(pos == 0, 0.0, pltpu.roll(z, 1, axis=0))
    z_next = jnp.where(pos == seq - 1, 0.0, pltpu.roll(z, tm - 1, axis=0))
    cw = cw_ref[...]
    y = cw[0:1, :] * z_prev + cw[1:2, :] * z + cw[2:3, :] * z_next
    o_ref[...] = (b * y).astype(o_ref.dtype)


def _conv_proj(h, w_conv, conv_w, seq, cast_src=None):
    r, d = h.shape
    d_conv = conv_w.shape[1]
    tm = _tile(r, max(seq, 1024))
    assert tm % seq == 0
    tc = _tile(d_conv, 256)
    nc = d_conv // tc

    def w_spec(sec):
        return pl.BlockSpec((d, tc), lambda i, j: (0, sec * nc + j))

    in_specs = [pl.BlockSpec((tm, d), lambda i, j: (i, 0)),
                w_spec(0), w_spec(1), w_spec(2),
                pl.BlockSpec((conv_w.shape[0], tc), lambda i, j: (0, j))]
    args = [h, w_conv, w_conv, w_conv, conv_w]
    out_specs = pl.BlockSpec((tm, tc), lambda i, j: (i, j))
    out_shape = jax.ShapeDtypeStruct((r, d_conv), _BF16)
    side_cast = None
    if cast_src is not None:
        side_cast = _SideCast(cast_src, r // tm, nc)
        in_specs.append(side_cast.spec)
        args.append(cast_src)
        out_specs, out_shape = [out_specs, side_cast.out_spec], [out_shape, side_cast.out_shape]
    return pl.pallas_call(
        functools.partial(_conv_kernel, seq=seq, side_cast=side_cast),
        grid=(r // tm, nc),
        in_specs=in_specs, out_specs=out_specs, out_shape=out_shape,
        compiler_params=_params("parallel", "arbitrary"),
        name="conv_proj",
    )(*args)


def _attn_kernel(*refs, n_heads, qk, lam_init, cached, side_cast):
    hw = 2 * qk
    if cached:
        q_ref, k_ref, v_ref, ck_ref, cv_ref, lam_ref, g_ref, o_ref, ckb_ref, cvb_ref = refs
        past = cv_ref.shape[0]

        @pl.when(pl.program_id(1) == 0)
        def _():
            for h in range(n_heads):
                for c in range(2):
                    rows = pl.ds(2 * h + c, past, stride=2 * n_heads)
                    ckb_ref[:, h * hw + c * qk:h * hw + (c + 1) * qk] = ck_ref[rows, :].astype(_BF16)
                cvb_ref[:, h * hw:(h + 1) * hw] = cv_ref[:, h, :].astype(_BF16)
    elif side_cast is not None:
        q_ref, k_ref, v_ref, lam_ref, g_ref, src_ref, o_ref, dst_ref = refs
        side_cast.run(src_ref, dst_ref)
    else:
        q_ref, k_ref, v_ref, lam_ref, g_ref, o_ref = refs
    lv = lam_ref[...]
    lam = (jnp.exp(jnp.sum(lv[0:1] * lv[1:2], axis=-1, keepdims=True))
           - jnp.exp(jnp.sum(lv[2:3] * lv[3:4], axis=-1, keepdims=True)) + lam_init)
    g = g_ref[...]
    dims = (((1,), (1,)), ((), ()))

    def probs(q, keys):
        s = [lax.dot_general(q, k, dims, preferred_element_type=_F32) for k in keys]
        m = functools.reduce(jnp.maximum, [jnp.max(x, axis=-1, keepdims=True) for x in s])
        e = [jnp.exp(x - m) for x in s]
        den = functools.reduce(jnp.add, [jnp.sum(x, axis=-1, keepdims=True) for x in e])
        return e, 1.0 / den

    for h in range(n_heads):
        o = None
        c0, c1 = h * hw, h * hw + qk
        keys0 = [k_ref[:, c0:c0 + qk]]
        keys1 = [k_ref[:, c1:c1 + qk]]
        vals = [v_ref[:, c0:c0 + hw]]
        if cached:
            keys0.append(ckb_ref[:, c0:c0 + qk])
            keys1.append(ckb_ref[:, c1:c1 + qk])
            vals.append(cvb_ref[:, c0:c0 + hw])
        e0, r0 = probs(q_ref[:, c0:c0 + qk], keys0)
        e1, r1 = probs(q_ref[:, c1:c1 + qk], keys1)
        r1 = r1 * lam
        for p0, p1, v in zip(e0, e1, vals):
            a = (p0 * r0 - p1 * r1).astype(_BF16)
            part = jnp.dot(a, v, preferred_element_type=_F32)
            o = part if o is None else o + part
        o = o * lax.rsqrt(jnp.mean(o * o, axis=-1, keepdims=True) + _EPS) * g * (1.0 - lam_init)
        o_ref[:, c0:c0 + hw] = o.astype(o_ref.dtype)


def _attention(qkv, lam_vecs, subln_g, n_heads, qk, seq, lam_init, cache=None, cast_src=None):
    r = qkv.shape[0]
    d_att = n_heads * 2 * qk
    tq = _tile(seq, 256)
    q_tiles = seq // tq
    in_specs = [pl.BlockSpec((tq, d_att), lambda b, t: (b * q_tiles + t, 0)),
                pl.BlockSpec((seq, d_att), lambda b, t: (b, 1)),
                pl.BlockSpec((seq, d_att), lambda b, t: (b, 2))]
    args = [qkv, qkv, qkv]
    if cache is not None:
        ck, cv = cache
        in_specs += [pl.BlockSpec((None,) + ck.shape[1:], lambda b, t: (b, 0, 0)),
                     pl.BlockSpec((None,) + cv.shape[1:], lambda b, t: (b, 0, 0, 0))]
        args += [ck, cv]
        scratch = [pltpu.VMEM((cv.shape[1], d_att), _BF16)] * 2
    else:
        scratch = []
    in_specs += [pl.BlockSpec(lam_vecs.shape, lambda b, t: (0, 0)),
                 pl.BlockSpec((1, 2 * qk), lambda b, t: (0, 0))]
    args += [lam_vecs, subln_g.reshape(1, 2 * qk)]
    out_specs = pl.BlockSpec((tq, d_att), lambda b, t: (b * q_tiles + t, 0))
    out_shape = jax.ShapeDtypeStruct((r, d_att), _BF16)
    side_cast = None
    if cast_src is not None:
        side_cast = _SideCast(cast_src, r // seq, q_tiles)
        in_specs.append(side_cast.spec)
        args.append(cast_src)
        out_specs, out_shape = [out_specs, side_cast.out_spec], [out_shape, side_cast.out_shape]
    return pl.pallas_call(
        functools.partial(_attn_kernel, n_heads=n_heads, qk=qk, lam_init=lam_init,
                          cached=cache is not None, side_cast=side_cast),
        grid=(r // seq, q_tiles),
        in_specs=in_specs, out_specs=out_specs, out_shape=out_shape,
        scratch_shapes=scratch,
        compiler_params=_params("parallel", "arbitrary"),
        name="diff_attention_latent" if cache is not None else "diff_attention_context",
    )(*args)


def _outproj_kernel(a_ref, c_ref, wa_ref, wc_ref, x_ref, mod_ref, *rest, gate_idx, side_cast):
    if side_cast is None:
        o_ref, = rest
    else:
        src_ref, o_ref, dst_ref = rest
        side_cast.run(src_ref, dst_ref)
    y = (jnp.dot(a_ref[...], wa_ref[...], preferred_element_type=_F32)
         + jnp.dot(c_ref[...], wc_ref[...], preferred_element_type=_F32))
    o_ref[...] = x_ref[...] + mod_ref[gate_idx:gate_idx + 1, :] * y


def _out_proj(attn, conv, w_out, x, mod3, gate_idx, cast_src=None):
    r, d = x.shape
    d_att, d_conv = attn.shape[1], conv.shape[1]
    assert d_att == d_conv
    rows_per_group = r // mod3.shape[0]
    tm = _tile(rows_per_group, 1024)
    tn = _tile(d, 512)
    in_specs = [pl.BlockSpec((tm, d_att), lambda i, j: (i, 0)),
                pl.BlockSpec((tm, d_conv), lambda i, j: (i, 0)),
                pl.BlockSpec((d_att, tn), lambda i, j: (0, j)),
                pl.BlockSpec((d_conv, tn), lambda i, j: (1, j)),
                pl.BlockSpec((tm, tn), lambda i, j: (i, j)),
                pl.BlockSpec((None, _N_MOD, tn), lambda i, j: ((i * tm) // rows_per_group, 0, j))]
    args = [attn, conv, w_out, w_out, x, mod3]
    out_specs = pl.BlockSpec((tm, tn), lambda i, j: (i, j))
    out_shape = jax.ShapeDtypeStruct((r, d), _F32)
    side_cast = None
    if cast_src is not None:
        side_cast = _SideCast(cast_src, r // tm, d // tn)
        in_specs.append(side_cast.spec)
        args.append(cast_src)
        out_specs, out_shape = [out_specs, side_cast.out_spec], [out_shape, side_cast.out_shape]
    return pl.pallas_call(
        functools.partial(_outproj_kernel, gate_idx=gate_idx, side_cast=side_cast),
        grid=(r // tm, d // tn),
        in_specs=in_specs, out_specs=out_specs, out_shape=out_shape,
        compiler_params=_params("parallel", "arbitrary"),
        name="out_proj",
    )(*args)


def _mlp_kernel(h_ref, w1_ref, w2_ref, x_ref, mod_ref, o_ref, hid_ref, *, gate_idx, n_f, tf):
    s = pl.program_id(1)

    @pl.when(s < n_f)
    def _():
        hid = jnp.dot(h_ref[...], w1_ref[...], preferred_element_type=_F32)
        col = pl.multiple_of(s * tf, tf)
        hid_ref[:, pl.ds(col, tf)] = jnp.square(jnp.maximum(hid, 0.0)).astype(hid_ref.dtype)

    @pl.when(s >= n_f)
    def _():
        y = jnp.dot(hid_ref[...], w2_ref[...], preferred_element_type=_F32)
        o_ref[...] = x_ref[...] + mod_ref[gate_idx:gate_idx + 1, :] * y


def _mlp(h, w1, w2, x, mod3, gate_idx):
    r, d = x.shape
    d_ff = w1.shape[1]
    rows_per_group = r // mod3.shape[0]
    tm = _tile(rows_per_group, 512)
    tf = _tile(d_ff, 1024)
    tn = _tile(d, 256)
    n_f = d_ff // tf

    def out_col(s):
        return jnp.maximum(s - n_f, 0)

    return pl.pallas_call(
        functools.partial(_mlp_kernel, gate_idx=gate_idx, n_f=n_f, tf=tf),
        grid=(r // tm, n_f + d // tn),
        in_specs=[pl.BlockSpec((tm, d), lambda i, s: (i, 0), pipeline_mode=pl.Buffered(1)),
                  pl.BlockSpec((d, tf), lambda i, s: (0, jnp.minimum(s, n_f - 1))),
                  pl.BlockSpec((d_ff, tn), lambda i, s: (0, out_col(s))),
                  pl.BlockSpec((tm, tn), lambda i, s: (i, out_col(s))),
                  pl.BlockSpec((None, _N_MOD, tn),
                               lambda i, s: ((i * tm) // rows_per_group, 0, out_col(s)))],
        out_specs=pl.BlockSpec((tm, tn), lambda i, s: (i, out_col(s))),
        out_shape=jax.ShapeDtypeStruct((r, d), _F32),
        scratch_shapes=[pltpu.VMEM((tm, d_ff), _BF16)],
        compiler_params=_params("parallel", "arbitrary"),
        name="mlp",
    )(h, w1, w2, x, mod3)


def _rope_tables(n_tokens, qk):
    axis_dim = qk // 2
    rows = n_tokens // _GRID_W
    row = jnp.repeat(jnp.arange(rows, dtype=_F32), _GRID_W)
    col = jnp.tile(jnp.arange(_GRID_W, dtype=_F32), rows)
    inv = jnp.power(_ROPE_THETA, -jnp.arange(0, axis_dim, 2, dtype=_F32) / axis_dim)
    row_ang, col_ang = row[:, None] * inv, col[:, None] * inv
    cos = jnp.concatenate([jnp.cos(row_ang)] * 2 + [jnp.cos(col_ang)] * 2, axis=-1)
    sin = jnp.concatenate([-jnp.sin(row_ang), jnp.sin(row_ang),
                           -jnp.sin(col_ang), jnp.sin(col_ang)], axis=-1)
    return cos, sin


def _trunk(x, mod3, seq, w, n_heads, qk, lam_init, rope=None, cache=None):
    d_att = n_heads * 2 * qk
    context = rope is None
    h = _prenorm(x, w["norm_attn_g"], mod3, 0, 1)
    if context:
        w = dict(w)
        assert w["w_in"].shape[1] == 2 * w["w_qkv"].shape[1]
        qkv, new_k, new_v, w["w_conv"] = _qkv_proj(h, w["w_qkv"], w["q_norm_g"], w["k_norm_g"], d_att, qk,
                                                   side_cast_args=(w["w_in"], 1, 2))
        conv, w["w_mlp_in"] = _conv_proj(h, w["w_conv"], w["conv_w"], seq, cast_src=w["w_mlp_in"])
        attn, w["w_out"] = _attention(qkv, w["lam_vecs"], w["subln_g"], n_heads, qk, seq, lam_init,
                                      cast_src=w["w_out"])
    else:
        qkv = _qkv_proj(h, w["w_qkv"], w["q_norm_g"], w["k_norm_g"], d_att, qk, rope)
        conv = _conv_proj(h, w["w_conv"], w["conv_w"], seq)
        attn = _attention(qkv, w["lam_vecs"], w["subln_g"], n_heads, qk, seq, lam_init, cache)
        new_k = new_v = None
    if context:
        x2, w["w_mlp_out"] = _out_proj(attn, conv, w["w_out"], x, mod3, 2, cast_src=w["w_mlp_out"])
    else:
        x2 = _out_proj(attn, conv, w["w_out"], x, mod3, 2)
    h2 = _prenorm(x2, w["norm_mlp_g"], mod3, 3, 4)
    y = _mlp(h2, w["w_mlp_in"], w["w_mlp_out"], x2, mod3, 5)
    return y, new_k, new_v, w


def kernel(x_prompt, x_sample, cache_k, cache_v, c, c_ctx, w_ada, b_ada, norm_attn_g, w_in, q_norm_g, k_norm_g, lambda_q1, lambda_k1, lambda_q2, lambda_k2, subln_g, conv_w, w_out, norm_mlp_g, w_mlp_in, w_mlp_out):
    assert w_in.shape[0] == 1, "one trunk layer"
    batch, seq, d = x_prompt.shape
    dec_batch, dec_seq, _ = x_sample.shape
    past, n_heads, _, qk = cache_k.shape[2:]
    d_att = n_heads * 2 * qk
    layer = 0
    lam_init = 0.8 - 0.6 * math.exp(-0.3 * layer)

    w = {
        "norm_attn_g": norm_attn_g[layer], "q_norm_g": q_norm_g[layer], "k_norm_g": k_norm_g[layer],
        "subln_g": subln_g[layer], "conv_w": conv_w[layer], "norm_mlp_g": norm_mlp_g[layer],
        "lam_vecs": jnp.stack([lambda_q1[layer], lambda_k1[layer], lambda_q2[layer], lambda_k2[layer]]),
        "w_in": w_in[layer], "w_qkv": w_in[layer][:, :3 * d_att].astype(_BF16), "w_out": w_out[layer],
        "w_mlp_in": w_mlp_in[layer], "w_mlp_out": w_mlp_out[layer],
    }

    n_cond = 1 + dec_batch
    cond = jnp.concatenate([c_ctx[None], c, jnp.zeros((-n_cond % 8, d), _F32)], axis=0)
    mod = _modulation(cond, w_ada[layer], b_ada[layer]).reshape(cond.shape[0], _N_MOD, d)

    y_p, new_k, new_v, w = _trunk(x_prompt.reshape(batch * seq, d), mod[0:1], seq, w,
                                  n_heads, qk, lam_init)
    cache = (cache_k[:, layer].reshape(dec_batch, past * n_heads * 2, qk), cache_v[:, layer])
    y_s = _trunk(x_sample.reshape(dec_batch * dec_seq, d), mod[1:n_cond], dec_seq, w,
                 n_heads, qk, lam_init, rope=_rope_tables(dec_seq, qk), cache=cache)[0]

    return (y_p.reshape(batch, seq, d),
            y_s.reshape(dec_batch, dec_seq, d),
            new_k.reshape(batch, 1, seq, n_heads, 2, qk),
            new_v.reshape(batch, 1, seq, n_heads, 2 * qk))
```

```python
import functools
import math

import jax
import jax.numpy as jnp
from jax import lax
from jax.experimental import pallas as pl
from jax.experimental.pallas import tpu as pltpu

_GRID_W = 64
_ROPE_THETA = 10000.0
_EPS = 1e-6
_N_MOD = 6
_LANES = 128
_VMEM_LIMIT = 60 * 1024 * 1024

_F32 = jnp.float32
_BF16 = jnp.bfloat16


def _tile(dim, pref):
    t = min(dim, pref)
    assert dim % t == 0, (dim, pref)
    return t


def _params(*sem):
    return pltpu.CompilerParams(dimension_semantics=sem, vmem_limit_bytes=_VMEM_LIMIT)


class _SideCast:
    def __init__(self, src, n_i, n_j, col_block=0, n_col_blocks=1):
        rows, cols = src.shape[0], src.shape[1] // n_col_blocks
        n_blocks = 1
        while n_blocks * 2 <= n_i * n_j and rows % (n_blocks * 2) == 0 and rows // (n_blocks * 2) >= 16:
            n_blocks *= 2
        self.src, self.n_j, self.n_blocks, self.all_steps = src, n_j, n_blocks, n_blocks == n_i * n_j
        block = (rows // n_blocks, cols)

        def row_block(*idx):
            step = idx[0] * n_j + idx[1] if len(idx) > 1 else idx[0]
            return jnp.minimum(step, n_blocks - 1)

        self.spec = pl.BlockSpec(block, lambda *idx: (row_block(*idx), col_block))
        self.out_spec = pl.BlockSpec(block, lambda *idx: (row_block(*idx), 0))
        self.out_shape = jax.ShapeDtypeStruct((rows, cols), _BF16)

    def run(self, src_ref, dst_ref):
        def cast():
            dst_ref[...] = src_ref[...].astype(dst_ref.dtype)
        if self.all_steps:
            cast()
        else:
            pl.when(pl.program_id(0) * self.n_j + pl.program_id(1) < self.n_blocks)(cast)


def _mod_kernel(cond_ref, w_ref, b_ref, o_ref):
    c = cond_ref[...]
    s = c / (1.0 + jnp.exp(-c))
    o_ref[...] = jnp.dot(s, w_ref[...], preferred_element_type=_F32) + b_ref[...]


def _modulation(cond, w_ada, b_ada):
    r, d = cond.shape
    n = w_ada.shape[1]
    tn = _tile(n, 512)
    return pl.pallas_call(
        _mod_kernel,
        grid=(n // tn,),
        in_specs=[pl.BlockSpec((r, d), lambda j: (0, 0)),
                  pl.BlockSpec((d, tn), lambda j: (0, j)),
                  pl.BlockSpec((1, tn), lambda j: (0, j))],
        out_specs=pl.BlockSpec((r, tn), lambda j: (0, j)),
        out_shape=jax.ShapeDtypeStruct((r, n), _F32),
        compiler_params=_params("arbitrary"),
        name="modulation",
    )(cond, w_ada, b_ada.reshape(1, n))


def _prenorm_kernel(x_ref, g_ref, mod_ref, o_ref, *, shift_idx, scale_idx):
    x = x_ref[...]
    gain = g_ref[...] * (1.0 + mod_ref[scale_idx:scale_idx + 1, :])
    shift = mod_ref[shift_idx:shift_idx + 1, :]
    r = lax.rsqrt(jnp.mean(x * x, axis=-1, keepdims=True) + _EPS)
    o_ref[...] = (x * r * gain + shift).astype(o_ref.dtype)


def _prenorm(x, g, mod3, shift_idx, scale_idx):
    r, d = x.shape
    rows_per_group = r // mod3.shape[0]
    tr = _tile(rows_per_group, 512)
    return pl.pallas_call(
        functools.partial(_prenorm_kernel, shift_idx=shift_idx, scale_idx=scale_idx),
        grid=(r // tr,),
        in_specs=[pl.BlockSpec((tr, d), lambda i: (i, 0)),
                  pl.BlockSpec((1, d), lambda i: (0, 0)),
                  pl.BlockSpec((None, _N_MOD, d), lambda i: ((i * tr) // rows_per_group, 0, 0))],
        out_specs=pl.BlockSpec((tr, d), lambda i: (i, 0)),
        out_shape=jax.ShapeDtypeStruct((r, d), _BF16),
        compiler_params=_params("parallel"),
        name="prenorm",
    )(x, g.reshape(1, d), mod3)


def _headnorm(xc, g, scale):
    r = lax.rsqrt(jnp.mean(xc * xc, axis=-1, keepdims=True) + _EPS)
    if scale != 1.0:
        r = r * scale
    return xc * r * g


def _rope(y, cos, sin_signed, first_half):
    qk = y.shape[-1]
    ahead = pltpu.roll(y, qk - qk // 4, axis=1)
    behind = pltpu.roll(y, qk // 4, axis=1)
    return y * cos + jnp.where(first_half, ahead, behind) * sin_signed


def _qkv_kernel(*refs, latent, qk, n_sec, q_scale, side_cast):
    if latent:
        x_ref, w_ref, qg_ref, kg_ref, cos_ref, sin_ref, qkv_ref = refs
    else:
        x_ref, w_ref, qg_ref, kg_ref, src_ref, qkv_ref, newk_ref, newv_ref, dst_ref = refs
        side_cast.run(src_ref, dst_ref)
    j = pl.program_id(1)
    tn = w_ref.shape[1]

    def project():
        return jnp.dot(x_ref[...], w_ref[...], preferred_element_type=_F32)

    if latent:
        shared = project()
        project = lambda: shared

    def qk_epilogue(g_ref, scale, f32_ref):
        acc = project()
        g = g_ref[...]
        if latent:
            cos = cos_ref[...]
            sin = sin_ref[...]
            lane = lax.broadcasted_iota(jnp.int32, (1, qk), 1)
            first_half = (lane % (qk // 2)) < (qk // 4)
        for c in range(tn // qk):
            cols = slice(c * qk, (c + 1) * qk)
            y = _headnorm(acc[:, cols], g, scale)
            if latent:
                y = _rope(y, cos, sin, first_half)
            if f32_ref is not None:
                n_chunks = n_sec * (tn // qk)
                chunk = (j - n_sec) * (tn // qk) + c
                f32_ref[pl.ds(chunk, acc.shape[0], stride=n_chunks), :] = y
            qkv_ref[:, cols] = y.astype(qkv_ref.dtype)

    @pl.when(j < n_sec)
    def _():
        qk_epilogue(qg_ref, q_scale, None)

    @pl.when((j >= n_sec) & (j < 2 * n_sec))
    def _():
        qk_epilogue(kg_ref, 1.0, None if latent else newk_ref)

    @pl.when(j >= 2 * n_sec)
    def _():
        acc = project()
        qkv_ref[...] = acc.astype(qkv_ref.dtype)
        if not latent:
            newv_ref[...] = acc


def _qkv_proj(h, w_qkv, q_g, k_g, d_att, qk, rope=None, side_cast_args=None):
    r, d = h.shape
    latent = rope is not None
    tm = _tile(r if not latent else rope[0].shape[0], 1024)
    tn = _tile(d_att, 512)
    n_sec = d_att // tn
    in_specs = [pl.BlockSpec((tm, d), lambda i, j: (i, 0)),
                pl.BlockSpec((d, tn), lambda i, j: (0, j)),
                pl.BlockSpec((1, qk), lambda i, j: (0, 0)),
                pl.BlockSpec((1, qk), lambda i, j: (0, 0))]
    args = [h, w_qkv, q_g.reshape(1, qk), k_g.reshape(1, qk)]
    qkv_spec = pl.BlockSpec((tm, tn), lambda i, j: (i, j))
    qkv_shape = jax.ShapeDtypeStruct((r, 3 * d_att), _BF16)
    if latent:
        seq_tiles = rope[0].shape[0] // tm
        rope_spec = pl.BlockSpec((tm, qk), lambda i, j: (i % seq_tiles, 0))
        in_specs += [rope_spec, rope_spec]
        args += list(rope)
        out_specs, out_shape = qkv_spec, qkv_shape
        side_cast = None
    else:
        side_cast = _SideCast(side_cast_args[0], r // tm, 3 * n_sec, *side_cast_args[1:])
        in_specs.append(side_cast.spec)
        args.append(side_cast.src)
        n_chunks = d_att // qk
        out_specs = [qkv_spec,
                     pl.BlockSpec((tm * n_chunks, qk), lambda i, j: (i, 0)),
                     pl.BlockSpec((tm, tn), lambda i, j: (i, jnp.clip(j - 2 * n_sec, 0, n_sec - 1))),
                     side_cast.out_spec]
        out_shape = [qkv_shape,
                     jax.ShapeDtypeStruct((r * n_chunks, qk), _F32),
                     jax.ShapeDtypeStruct((r, d_att), _F32),
                     side_cast.out_shape]
    return pl.pallas_call(
        functools.partial(_qkv_kernel, latent=latent, qk=qk, n_sec=n_sec, q_scale=qk ** -0.5,
                          side_cast=side_cast),
        grid=(r // tm, 3 * n_sec),
        in_specs=in_specs, out_specs=out_specs, out_shape=out_shape,
        compiler_params=_params("parallel", "arbitrary"),
        name="qkv_proj_latent" if latent else "qkv_proj_context",
    )(*args)


def _conv_kernel(x_ref, wb_ref, wc_ref, wu_ref, cw_ref, *rest, seq, side_cast):
    if side_cast is None:
        o_ref, = rest
    else:
        src_ref, o_ref, dst_ref = rest
        side_cast.run(src_ref, dst_ref)
    x = x_ref[...]
    b = jnp.dot(x, wb_ref[...], preferred_element_type=_F32)
    c = jnp.dot(x, wc_ref[...], preferred_element_type=_F32)
    u = jnp.dot(x, wu_ref[...], preferred_element_type=_F32)
    z = c * u
    tm = z.shape[0]
    pos = lax.broadcasted_iota(jnp.int32, (tm, 1), 0) % seq
    z_prev = jnp.where(pos == 0, 0.0, pltpu.roll(z, 1, axis=0))
    z_next = jnp.where(pos == seq - 1, 0.0, pltpu.roll(z, tm - 1, axis=0))
    cw = cw_ref[...]
    y = cw[0:1, :] * z_prev + cw[1:2, :] * z + cw[2:3, :] * z_next
    o_ref[...] = (b * y).astype(o_ref.dtype)


def _conv_proj(h, w_conv, conv_w, seq, cast_src=None):
    r, d = h.shape
    d_conv = conv_w.shape[1]
    tm = _tile(r, max(seq, 1024))
    assert tm % seq == 0
    tc = _tile(d_conv, 256 if cast_src is not None else 512)
    nc = d_conv // tc

    def w_spec(sec):
        return pl.BlockSpec((d, tc), lambda i, j: (0, sec * nc + j))

    in_specs = [pl.BlockSpec((tm, d), lambda i, j: (i, 0)),
                w_spec(0), w_spec(1), w_spec(2),
                pl.BlockSpec((conv_w.shape[0], tc), lambda i, j: (0, j))]
    args = [h, w_conv, w_conv, w_conv, conv_w]
    out_specs = pl.BlockSpec((tm, tc), lambda i, j: (i, j))
    out_shape = jax.ShapeDtypeStruct((r, d_conv), _BF16)
    side_cast = None
    if cast_src is not None:
        side_cast = _SideCast(cast_src, r // tm, nc)
        in_specs.append(side_cast.spec)
        args.append(cast_src)
        out_specs, out_shape = [out_specs, side_cast.out_spec], [out_shape, side_cast.out_shape]
    return pl.pallas_call(
        functools.partial(_conv_kernel, seq=seq, side_cast=side_cast),
        grid=(r // tm, nc),
        in_specs=in_specs, out_specs=out_specs, out_shape=out_shape,
        compiler_params=_params("parallel", "arbitrary"),
        name="conv_proj",
    )(*args)


def _attn_kernel(*refs, n_heads, qk, lam_init, cached, side_cast):
    hw = 2 * qk
    if cached:
        q_ref, k_ref, v_ref, ck_ref, cv_ref, lam_ref, g_ref, o_ref, ckb_ref, cvb_ref = refs
        past = cv_ref.shape[0]

        @pl.when(pl.program_id(1) == 0)
        def _():
            for h in range(n_heads):
                for c in range(2):
                    rows = pl.ds(2 * h + c, past, stride=2 * n_heads)
                    ckb_ref[:, h * hw + c * qk:h * hw + (c + 1) * qk] = ck_ref[rows, :].astype(_BF16)
                cvb_ref[:, h * hw:(h + 1) * hw] = cv_ref[:, h, :].astype(_BF16)
    elif side_cast is not None:
        q_ref, k_ref, v_ref, lam_ref, g_ref, src_ref, o_ref, dst_ref = refs
        side_cast.run(src_ref, dst_ref)
    else:
        q_ref, k_ref, v_ref, lam_ref, g_ref, o_ref = refs
    lv = lam_ref[...]
    lam = (jnp.exp(jnp.sum(lv[0:1] * lv[1:2], axis=-1, keepdims=True))
           - jnp.exp(jnp.sum(lv[2:3] * lv[3:4], axis=-1, keepdims=True)) + lam_init)
    g = g_ref[...]
    dims = (((1,), (1,)), ((), ()))

    def probs(q, keys):
        s = [lax.dot_general(q, k, dims, preferred_element_type=_F32) for k in keys]
        m = functools.reduce(jnp.maximum, [jnp.max(x, axis=-1, keepdims=True) for x in s])
        e = [jnp.exp(x - m) for x in s]
        den = functools.reduce(jnp.add, [jnp.sum(x, axis=-1, keepdims=True) for x in e])
        return e, 1.0 / den

    for h in range(n_heads):
        o = None
        c0, c1 = h * hw, h * hw + qk
        keys0 = [k_ref[:, c0:c0 + qk]]
        keys1 = [k_ref[:, c1:c1 + qk]]
        vals = [v_ref[:, c0:c0 + hw]]
        if cached:
            keys0.append(ckb_ref[:, c0:c0 + qk])
            keys1.append(ckb_ref[:, c1:c1 + qk])
            vals.append(cvb_ref[:, c0:c0 + hw])
        e0, r0 = probs(q_ref[:, c0:c0 + qk], keys0)
        e1, r1 = probs(q_ref[:, c1:c1 + qk], keys1)
        r1 = r1 * lam
        for p0, p1, v in zip(e0, e1, vals):
            a = (p0 * r0 - p1 * r1).astype(_BF16)
            part = jnp.dot(a, v, preferred_element_type=_F32)
            o = part if o is None else o + part
        o = o * lax.rsqrt(jnp.mean(o * o, axis=-1, keepdims=True) + _EPS) * g * (1.0 - lam_init)
        o_ref[:, c0:c0 + hw] = o.astype(o_ref.dtype)


def _attention(qkv, lam_vecs, subln_g, n_heads, qk, seq, lam_init, cache=None, cast_src=None):
    r = qkv.shape[0]
    d_att = n_heads * 2 * qk
    tq = _tile(seq, 256)
    q_tiles = seq // tq
    in_specs = [pl.BlockSpec((tq, d_att), lambda b, t: (b * q_tiles + t, 0)),
                pl.BlockSpec((seq, d_att), lambda b, t: (b, 1)),
                pl.BlockSpec((seq, d_att), lambda b, t: (b, 2))]
    args = [qkv, qkv, qkv]
    if cache is not None:
        ck, cv = cache
        in_specs += [pl.BlockSpec((None,) + ck.shape[1:], lambda b, t: (b, 0, 0)),
                     pl.BlockSpec((None,) + cv.shape[1:], lambda b, t: (b, 0, 0, 0))]
        args += [ck, cv]
        scratch = [pltpu.VMEM((cv.shape[1], d_att), _BF16)] * 2
    else:
        scratch = []
    in_specs += [pl.BlockSpec(lam_vecs.shape, lambda b, t: (0, 0)),
                 pl.BlockSpec((1, 2 * qk), lambda b, t: (0, 0))]
    args += [lam_vecs, subln_g.reshape(1, 2 * qk)]
    out_specs = pl.BlockSpec((tq, d_att), lambda b, t: (b * q_tiles + t, 0))
    out_shape = jax.ShapeDtypeStruct((r, d_att), _BF16)
    side_cast = None
    if cast_src is not None:
        side_cast = _SideCast(cast_src, r // seq, q_tiles)
        in_specs.append(side_cast.spec)
        args.append(cast_src)
        out_specs, out_shape = [out_specs, side_cast.out_spec], [out_shape, side_cast.out_shape]
    return pl.pallas_call(
        functools.partial(_attn_kernel, n_heads=n_heads, qk=qk, lam_init=lam_init,
                          cached=cache is not None, side_cast=side_cast),
        grid=(r // seq, q_tiles),
        in_specs=in_specs, out_specs=out_specs, out_shape=out_shape,
        scratch_shapes=scratch,
        compiler_params=_params("parallel", "arbitrary"),
        name="diff_attention_latent" if cache is not None else "diff_attention_context",
    )(*args)


def _outproj_kernel(a_ref, c_ref, wa_ref, wc_ref, x_ref, mod_ref, *rest, gate_idx, side_cast):
    if side_cast is None:
        o_ref, = rest
    else:
        src_ref, o_ref, dst_ref = rest
        side_cast.run(src_ref, dst_ref)
    y = (jnp.dot(a_ref[...], wa_ref[...], preferred_element_type=_F32)
         + jnp.dot(c_ref[...], wc_ref[...], preferred_element_type=_F32))
    o_ref[...] = x_ref[...] + mod_ref[gate_idx:gate_idx + 1, :] * y


def _out_proj(attn, conv, w_out, x, mod3, gate_idx, cast_src=None):
    r, d = x.shape
    d_att, d_conv = attn.shape[1], conv.shape[1]
    assert d_att == d_conv
    rows_per_group = r // mod3.shape[0]
    tm = _tile(rows_per_group, 1024)
    tn = _tile(d, 512 if cast_src is not None else 1024)
    in_specs = [pl.BlockSpec((tm, d_att), lambda i, j: (i, 0)),
                pl.BlockSpec((tm, d_conv), lambda i, j: (i, 0)),
                pl.BlockSpec((d_att, tn), lambda i, j: (0, j)),
                pl.BlockSpec((d_conv, tn), lambda i, j: (1, j)),
                pl.BlockSpec((tm, tn), lambda i, j: (i, j)),
                pl.BlockSpec((None, _N_MOD, tn), lambda i, j: ((i * tm) // rows_per_group, 0, j))]
    args = [attn, conv, w_out, w_out, x, mod3]
    out_specs = pl.BlockSpec((tm, tn), lambda i, j: (i, j))
    out_shape = jax.ShapeDtypeStruct((r, d), _F32)
    side_cast = None
    if cast_src is not None:
        side_cast = _SideCast(cast_src, r // tm, d // tn)
        in_specs.append(side_cast.spec)
        args.append(cast_src)
        out_specs, out_shape = [out_specs, side_cast.out_spec], [out_shape, side_cast.out_shape]
    return pl.pallas_call(
        functools.partial(_outproj_kernel, gate_idx=gate_idx, side_cast=side_cast),
        grid=(r // tm, d // tn),
        in_specs=in_specs, out_specs=out_specs, out_shape=out_shape,
        compiler_params=_params("parallel", "arbitrary"),
        name="out_proj",
    )(*args)


def _mlp_kernel(h_ref, w1_ref, w2_ref, x_ref, mod_ref, o_ref, hid_ref, *, gate_idx, n_f, tf):
    s = pl.program_id(1)

    @pl.when(s < n_f)
    def _():
        hid = jnp.dot(h_ref[...], w1_ref[...], preferred_element_type=_F32)
        col = pl.multiple_of(s * tf, tf)
        hid_ref[:, pl.ds(col, tf)] = jnp.square(jnp.maximum(hid, 0.0)).astype(hid_ref.dtype)

    @pl.when(s >= n_f)
    def _():
        y = jnp.dot(hid_ref[...], w2_ref[...], preferred_element_type=_F32)
        o_ref[...] = x_ref[...] + mod_ref[gate_idx:gate_idx + 1, :] * y


def _mlp(h, w1, w2, x, mod3, gate_idx):
    r, d = x.shape
    d_ff = w1.shape[1]
    rows_per_group = r // mod3.shape[0]
    tm = _tile(rows_per_group, 512)
    tf = _tile(d_ff, 1024)
    tn = _tile(d, 256)
    n_f = d_ff // tf

    def out_col(s):
        return jnp.maximum(s - n_f, 0)

    return pl.pallas_call(
        functools.partial(_mlp_kernel, gate_idx=gate_idx, n_f=n_f, tf=tf),
        grid=(r // tm, n_f + d // tn),
        in_specs=[pl.BlockSpec((tm, d), lambda i, s: (i, 0), pipeline_mode=pl.Buffered(1)),
                  pl.BlockSpec((d, tf), lambda i, s: (0, jnp.minimum(s, n_f - 1))),
                  pl.BlockSpec((d_ff, tn), lambda i, s: (0, out_col(s))),
                  pl.BlockSpec((tm, tn), lambda i, s: (i, out_col(s))),
                  pl.BlockSpec((None, _N_MOD, tn),
                               lambda i, s: ((i * tm) // rows_per_group, 0, out_col(s)))],
        out_specs=pl.BlockSpec((tm, tn), lambda i, s: (i, out_col(s))),
        out_shape=jax.ShapeDtypeStruct((r, d), _F32),
        scratch_shapes=[pltpu.VMEM((tm, d_ff), _BF16)],
        compiler_params=_params("parallel", "arbitrary"),
        name="mlp",
    )(h, w1, w2, x, mod3)


def _rope_tables(n_tokens, qk):
    axis_dim = qk // 2
    rows = n_tokens // _GRID_W
    row = jnp.repeat(jnp.arange(rows, dtype=_F32), _GRID_W)
    col = jnp.tile(jnp.arange(_GRID_W, dtype=_F32), rows)
    inv = jnp.power(_ROPE_THETA, -jnp.arange(0, axis_dim, 2, dtype=_F32) / axis_dim)
    row_ang, col_ang = row[:, None] * inv, col[:, None] * inv
    cos = jnp.concatenate([jnp.cos(row_ang)] * 2 + [jnp.cos(col_ang)] * 2, axis=-1)
    sin = jnp.concatenate([-jnp.sin(row_ang), jnp.sin(row_ang),
                           -jnp.sin(col_ang), jnp.sin(col_ang)], axis=-1)
    return cos, sin


def _trunk(x, mod3, seq, w, n_heads, qk, lam_init, rope=None, cache=None):
    d_att = n_heads * 2 * qk
    context = rope is None
    h = _prenorm(x, w["norm_attn_g"], mod3, 0, 1)
    if context:
        w = dict(w)
        assert w["w_in"].shape[1] == 2 * w["w_qkv"].shape[1]
        qkv, new_k, new_v, w["w_conv"] = _qkv_proj(h, w["w_qkv"], w["q_norm_g"], w["k_norm_g"], d_att, qk,
                                                   side_cast_args=(w["w_in"], 1, 2))
        conv, w["w_mlp_in"] = _conv_proj(h, w["w_conv"], w["conv_w"], seq, cast_src=w["w_mlp_in"])
        attn, w["w_out"] = _attention(qkv, w["lam_vecs"], w["subln_g"], n_heads, qk, seq, lam_init,
                                      cast_src=w["w_out"])
    else:
        qkv = _qkv_proj(h, w["w_qkv"], w["q_norm_g"], w["k_norm_g"], d_att, qk, rope)
        conv = _conv_proj(h, w["w_conv"], w["conv_w"], seq)
        attn = _attention(qkv, w["lam_vecs"], w["subln_g"], n_heads, qk, seq, lam_init, cache)
        new_k = new_v = None
    if context:
        x2, w["w_mlp_out"] = _out_proj(attn, conv, w["w_out"], x, mod3, 2, cast_src=w["w_mlp_out"])
    else:
        x2 = _out_proj(attn, conv, w["w_out"], x, mod3, 2)
    h2 = _prenorm(x2, w["norm_mlp_g"], mod3, 3, 4)
    y = _mlp(h2, w["w_mlp_in"], w["w_mlp_out"], x2, mod3, 5)
    return y, new_k, new_v, w


def kernel(x_prompt, x_sample, cache_k, cache_v, c, c_ctx, w_ada, b_ada, norm_attn_g, w_in, q_norm_g, k_norm_g, lambda_q1, lambda_k1, lambda_q2, lambda_k2, subln_g, conv_w, w_out, norm_mlp_g, w_mlp_in, w_mlp_out):
    assert w_in.shape[0] == 1, "one trunk layer"
    batch, seq, d = x_prompt.shape
    dec_batch, dec_seq, _ = x_sample.shape
    past, n_heads, _, qk = cache_k.shape[2:]
    d_att = n_heads * 2 * qk
    layer = 0
    lam_init = 0.8 - 0.6 * math.exp(-0.3 * layer)

    w = {
        "norm_attn_g": norm_attn_g[layer], "q_norm_g": q_norm_g[layer], "k_norm_g": k_norm_g[layer],
        "subln_g": subln_g[layer], "conv_w": conv_w[layer], "norm_mlp_g": norm_mlp_g[layer],
        "lam_vecs": jnp.stack([lambda_q1[layer], lambda_k1[layer], lambda_q2[layer], lambda_k2[layer]]),
        "w_in": w_in[layer], "w_qkv": w_in[layer][:, :3 * d_att].astype(_BF16), "w_out": w_out[layer],
        "w_mlp_in": w_mlp_in[layer], "w_mlp_out": w_mlp_out[layer],
    }

    n_cond = 1 + dec_batch
    cond = jnp.concatenate([c_ctx[None], c, jnp.zeros((-n_cond % 8, d), _F32)], axis=0)
    mod = _modulation(cond, w_ada[layer], b_ada[layer]).reshape(cond.shape[0], _N_MOD, d)

    y_p, new_k, new_v, w = _trunk(x_prompt.reshape(batch * seq, d), mod[0:1], seq, w,
                                  n_heads, qk, lam_init)
    cache = (cache_k[:, layer].reshape(dec_batch, past * n_heads * 2, qk), cache_v[:, layer])
    x_sample, w["w_mlp_in"] = lax.optimization_barrier((x_sample, w["w_mlp_in"]))
    y_s = _trunk(x_sample.reshape(dec_batch * dec_seq, d), mod[1:n_cond], dec_seq, w,
                 n_heads, qk, lam_init, rope=_rope_tables(dec_seq, qk), cache=cache)[0]

    return (y_p.reshape(batch, seq, d),
            y_s.reshape(dec_batch, dec_seq, d),
            new_k.reshape(batch, 1, seq, n_heads, 2, qk),
            new_v.reshape(batch, 1, seq, n_heads, 2 * qk))
```

```python
import functools
import math

import jax
import jax.numpy as jnp
from jax import lax
from jax.experimental import pallas as pl
from jax.experimental.pallas import tpu as pltpu

_GRID_W = 64
_ROPE_THETA = 10000.0
_EPS = 1e-6
_N_MOD = 6
_LANES = 128
_VMEM_LIMIT = 63 * 1024 * 1024

_F32 = jnp.float32
_BF16 = jnp.bfloat16


def _tile(dim, pref):
    t = min(dim, pref)
    assert dim % t == 0, (dim, pref)
    return t


def _params(*sem):
    return pltpu.CompilerParams(dimension_semantics=sem, vmem_limit_bytes=_VMEM_LIMIT)


class _SideCast:
    def __init__(self, src, n_i, n_j, col_block=0, n_col_blocks=1):
        rows, cols = src.shape[0], src.shape[1] // n_col_blocks
        n_blocks = 1
        while n_blocks * 2 <= n_i * n_j and rows % (n_blocks * 2) == 0 and rows // (n_blocks * 2) >= 16:
            n_blocks *= 2
        self.src, self.n_j, self.n_blocks, self.all_steps = src, n_j, n_blocks, n_blocks == n_i * n_j
        block = (rows // n_blocks, cols)

        def row_block(*idx):
            step = idx[0] * n_j + idx[1] if len(idx) > 1 else idx[0]
            return jnp.minimum(step, n_blocks - 1)

        self.spec = pl.BlockSpec(block, lambda *idx: (row_block(*idx), col_block))
        self.out_spec = pl.BlockSpec(block, lambda *idx: (row_block(*idx), 0))
        self.out_shape = jax.ShapeDtypeStruct((rows, cols), _BF16)

    def run(self, src_ref, dst_ref):
        def cast():
            dst_ref[...] = src_ref[...].astype(dst_ref.dtype)
        if self.all_steps:
            cast()
        else:
            pl.when(pl.program_id(0) * self.n_j + pl.program_id(1) < self.n_blocks)(cast)


def _mod_kernel(cond_ref, w_ref, b_ref, o_ref):
    c = cond_ref[...]
    s = c / (1.0 + jnp.exp(-c))
    o_ref[...] = jnp.dot(s, w_ref[...], preferred_element_type=_F32) + b_ref[...]


def _modulation(cond, w_ada, b_ada):
    r, d = cond.shape
    n = w_ada.shape[1]
    tn = _tile(n, 512)
    return pl.pallas_call(
        _mod_kernel,
        grid=(n // tn,),
        in_specs=[pl.BlockSpec((r, d), lambda j: (0, 0)),
                  pl.BlockSpec((d, tn), lambda j: (0, j)),
                  pl.BlockSpec((1, tn), lambda j: (0, j))],
        out_specs=pl.BlockSpec((r, tn), lambda j: (0, j)),
        out_shape=jax.ShapeDtypeStruct((r, n), _F32),
        compiler_params=_params("arbitrary"),
        name="modulation",
    )(cond, w_ada, b_ada.reshape(1, n))


def _prenorm_kernel(x_ref, g_ref, mod_ref, o_ref, *, shift_idx, scale_idx):
    x = x_ref[...]
    gain = g_ref[...] * (1.0 + mod_ref[scale_idx:scale_idx + 1, :])
    shift = mod_ref[shift_idx:shift_idx + 1, :]
    r = lax.rsqrt(jnp.mean(x * x, axis=-1, keepdims=True) + _EPS)
    o_ref[...] = (x * r * gain + shift).astype(o_ref.dtype)


def _prenorm(x, g, mod3, shift_idx, scale_idx):
    r, d = x.shape
    rows_per_group = r // mod3.shape[0]
    tr = _tile(rows_per_group, 512)
    return pl.pallas_call(
        functools.partial(_prenorm_kernel, shift_idx=shift_idx, scale_idx=scale_idx),
        grid=(r // tr,),
        in_specs=[pl.BlockSpec((tr, d), lambda i: (i, 0)),
                  pl.BlockSpec((1, d), lambda i: (0, 0)),
                  pl.BlockSpec((None, _N_MOD, d), lambda i: ((i * tr) // rows_per_group, 0, 0))],
        out_specs=pl.BlockSpec((tr, d), lambda i: (i, 0)),
        out_shape=jax.ShapeDtypeStruct((r, d), _BF16),
        compiler_params=_params("parallel"),
        name="prenorm",
    )(x, g.reshape(1, d), mod3)


def _headnorm(xc, g, scale):
    r = lax.rsqrt(jnp.mean(xc * xc, axis=-1, keepdims=True) + _EPS)
    if scale != 1.0:
        r = r * scale
    return xc * r * g


def _rope(y, cos, sin_signed, first_half):
    qk = y.shape[-1]
    ahead = pltpu.roll(y, qk - qk // 4, axis=1)
    behind = pltpu.roll(y, qk // 4, axis=1)
    return y * cos + jnp.where(first_half, ahead, behind) * sin_signed


def _qkv_kernel(*refs, latent, qk, n_sec, q_scale, side_cast):
    if latent:
        x_ref, w_ref, qg_ref, kg_ref, cos_ref, sin_ref, qkv_ref = refs
    else:
        x_ref, w_ref, qg_ref, kg_ref, src_ref, qkv_ref, newk_ref, newv_ref, dst_ref = refs
        side_cast.run(src_ref, dst_ref)
    j = pl.program_id(1)
    tn = w_ref.shape[1]

    def project():
        return jnp.dot(x_ref[...], w_ref[...], preferred_element_type=_F32)

    if latent:
        shared = project()
        project = lambda: shared

    def qk_epilogue(g_ref, scale, f32_ref):
        acc = project()
        g = g_ref[...]
        if latent:
            cos = cos_ref[...]
            sin = sin_ref[...]
            lane = lax.broadcasted_iota(jnp.int32, (1, qk), 1)
            first_half = (lane % (qk // 2)) < (qk // 4)
        for c in range(tn // qk):
            cols = slice(c * qk, (c + 1) * qk)
            y = _headnorm(acc[:, cols], g, scale)
            if latent:
                y = _rope(y, cos, sin, first_half)
            if f32_ref is not None:
                n_chunks = n_sec * (tn // qk)
                chunk = (j - n_sec) * (tn // qk) + c
                f32_ref[pl.ds(chunk, acc.shape[0], stride=n_chunks), :] = y
            qkv_ref[:, cols] = y.astype(qkv_ref.dtype)

    @pl.when(j < n_sec)
    def _():
        qk_epilogue(qg_ref, q_scale, None)

    @pl.when((j >= n_sec) & (j < 2 * n_sec))
    def _():
        qk_epilogue(kg_ref, 1.0, None if latent else newk_ref)

    @pl.when(j >= 2 * n_sec)
    def _():
        acc = project()
        qkv_ref[...] = acc.astype(qkv_ref.dtype)
        if not latent:
            newv_ref[...] = acc


def _qkv_proj(h, w_qkv, q_g, k_g, d_att, qk, rope=None, side_cast_args=None):
    r, d = h.shape
    latent = rope is not None
    tm = _tile(r if not latent else rope[0].shape[0], 1024)
    tn = _tile(d_att, 512)
    n_sec = d_att // tn
    in_specs = [pl.BlockSpec((tm, d), lambda i, j: (i, 0)),
                pl.BlockSpec((d, tn), lambda i, j: (0, j)),
                pl.BlockSpec((1, qk), lambda i, j: (0, 0)),
                pl.BlockSpec((1, qk), lambda i, j: (0, 0))]
    args = [h, w_qkv, q_g.reshape(1, qk), k_g.reshape(1, qk)]
    qkv_spec = pl.BlockSpec((tm, tn), lambda i, j: (i, j))
    qkv_shape = jax.ShapeDtypeStruct((r, 3 * d_att), _BF16)
    if latent:
        seq_tiles = rope[0].shape[0] // tm
        rope_spec = pl.BlockSpec((tm, qk), lambda i, j: (i % seq_tiles, 0))
        in_specs += [rope_spec, rope_spec]
        args += list(rope)
        out_specs, out_shape = qkv_spec, qkv_shape
        side_cast = None
    else:
        side_cast = _SideCast(side_cast_args[0], r // tm, 3 * n_sec, *side_cast_args[1:])
        in_specs.append(side_cast.spec)
        args.append(side_cast.src)
        n_chunks = d_att // qk
        out_specs = [qkv_spec,
                     pl.BlockSpec((tm * n_chunks, qk), lambda i, j: (i, 0)),
                     pl.BlockSpec((tm, tn), lambda i, j: (i, jnp.clip(j - 2 * n_sec, 0, n_sec - 1))),
                     side_cast.out_spec]
        out_shape = [qkv_shape,
                     jax.ShapeDtypeStruct((r * n_chunks, qk), _F32),
                     jax.ShapeDtypeStruct((r, d_att), _F32),
                     side_cast.out_shape]
    return pl.pallas_call(
        functools.partial(_qkv_kernel, latent=latent, qk=qk, n_sec=n_sec, q_scale=qk ** -0.5,
                          side_cast=side_cast),
        grid=(r // tm, 3 * n_sec),
        in_specs=in_specs, out_specs=out_specs, out_shape=out_shape,
        compiler_params=_params("parallel", "arbitrary"),
        name="qkv_proj_latent" if latent else "qkv_proj_context",
    )(*args)


def _conv_kernel(x_ref, wb_ref, wc_ref, wu_ref, cw_ref, *rest, seq, side_cast):
    if side_cast is None:
        o_ref, = rest
    else:
        src_ref, o_ref, dst_ref = rest
        side_cast.run(src_ref, dst_ref)
    x = x_ref[...]
    b = jnp.dot(x, wb_ref[...], preferred_element_type=_F32)
    c = jnp.dot(x, wc_ref[...], preferred_element_type=_F32)
    u = jnp.dot(x, wu_ref[...], preferred_element_type=_F32)
    z = c * u
    tm = z.shape[0]
    pos = lax.broadcasted_iota(jnp.int32, (tm, 1), 0) % seq
    z_prev = jnp.where(pos == 0, 0.0, pltpu.roll(z, 1, axis=0))
    z_next = jnp.where(pos == seq - 1, 0.0, pltpu.roll(z, tm - 1, axis=0))
    cw = cw_ref[...]
    y = cw[0:1, :] * z_prev + cw[1:2, :] * z + cw[2:3, :] * z_next
    o_ref[...] = (b * y).astype(o_ref.dtype)


def _conv_proj(h, w_conv, conv_w, seq, cast_src=None):
    r, d = h.shape
    d_conv = conv_w.shape[1]
    tm = _tile(r, max(seq, 1024))
    assert tm % seq == 0
    tc = _tile(d_conv, 256 if cast_src is not None else 512)
    nc = d_conv // tc

    def w_spec(sec):
        return pl.BlockSpec((d, tc), lambda i, j: (0, sec * nc + j))

    in_specs = [pl.BlockSpec((tm, d), lambda i, j: (i, 0)),
                w_spec(0), w_spec(1), w_spec(2),
                pl.BlockSpec((conv_w.shape[0], tc), lambda i, j: (0, j))]
    args = [h, w_conv, w_conv, w_conv, conv_w]
    out_specs = pl.BlockSpec((tm, tc), lambda i, j: (i, j))
    out_shape = jax.ShapeDtypeStruct((r, d_conv), _BF16)
    side_cast = None
    if cast_src is not None:
        side_cast = _SideCast(cast_src, r // tm, nc)
        in_specs.append(side_cast.spec)
        args.append(cast_src)
        out_specs, out_shape = [out_specs, side_cast.out_spec], [out_shape, side_cast.out_shape]
    return pl.pallas_call(
        functools.partial(_conv_kernel, seq=seq, side_cast=side_cast),
        grid=(r // tm, nc),
        in_specs=in_specs, out_specs=out_specs, out_shape=out_shape,
        compiler_params=_params("parallel", "arbitrary"),
        name="conv_proj",
    )(*args)


def _attn_kernel(*refs, n_heads, qk, lam_init, cached, side_cast):
    hw = 2 * qk
    if cached:
        q_ref, k_ref, v_ref, ck_ref, cv_ref, lam_ref, g_ref, o_ref, ckb_ref, cvb_ref = refs
        past = cv_ref.shape[0]

        @pl.when(pl.program_id(1) == 0)
        def _():
            for h in range(n_heads):
                for c in range(2):
                    rows = pl.ds(2 * h + c, past, stride=2 * n_heads)
                    ckb_ref[:, h * hw + c * qk:h * hw + (c + 1) * qk] = ck_ref[rows, :].astype(_BF16)
                cvb_ref[:, h * hw:(h + 1) * hw] = cv_ref[:, h, :].astype(_BF16)
    elif side_cast is not None:
        q_ref, k_ref, v_ref, lam_ref, g_ref, src_ref, o_ref, dst_ref = refs
        side_cast.run(src_ref, dst_ref)
    else:
        q_ref, k_ref, v_ref, lam_ref, g_ref, o_ref = refs
    lv = lam_ref[...]
    lam = (jnp.exp(jnp.sum(lv[0:1] * lv[1:2], axis=-1, keepdims=True))
           - jnp.exp(jnp.sum(lv[2:3] * lv[3:4], axis=-1, keepdims=True)) + lam_init)
    g = g_ref[...]
    dims = (((1,), (1,)), ((), ()))

    def probs(q, keys):
        s = [lax.dot_general(q, k, dims, preferred_element_type=_F32) for k in keys]
        m = functools.reduce(jnp.maximum, [jnp.max(x, axis=-1, keepdims=True) for x in s])
        e = [jnp.exp(x - m) for x in s]
        den = functools.reduce(jnp.add, [jnp.sum(x, axis=-1, keepdims=True) for x in e])
        return e, 1.0 / den

    for h in range(n_heads):
        o = None
        c0, c1 = h * hw, h * hw + qk
        keys0 = [k_ref[:, c0:c0 + qk]]
        keys1 = [k_ref[:, c1:c1 + qk]]
        vals = [v_ref[:, c0:c0 + hw]]
        if cached:
            keys0.append(ckb_ref[:, c0:c0 + qk])
            keys1.append(ckb_ref[:, c1:c1 + qk])
            vals.append(cvb_ref[:, c0:c0 + hw])
        e0, r0 = probs(q_ref[:, c0:c0 + qk], keys0)
        e1, r1 = probs(q_ref[:, c1:c1 + qk], keys1)
        r1 = r1 * lam
        for p0, p1, v in zip(e0, e1, vals):
            a = (p0 * r0 - p1 * r1).astype(_BF16)
            part = jnp.dot(a, v, preferred_element_type=_F32)
            o = part if o is None else o + part
        o = o * lax.rsqrt(jnp.mean(o * o, axis=-1, keepdims=True) + _EPS) * g * (1.0 - lam_init)
        o_ref[:, c0:c0 + hw] = o.astype(o_ref.dtype)


def _attention(qkv, lam_vecs, subln_g, n_heads, qk, seq, lam_init, cache=None, cast_src=None):
    r = qkv.shape[0]
    d_att = n_heads * 2 * qk
    tq = _tile(seq, 256)
    q_tiles = seq // tq
    in_specs = [pl.BlockSpec((tq, d_att), lambda b, t: (b * q_tiles + t, 0)),
                pl.BlockSpec((seq, d_att), lambda b, t: (b, 1)),
                pl.BlockSpec((seq, d_att), lambda b, t: (b, 2))]
    args = [qkv, qkv, qkv]
    if cache is not None:
        ck, cv = cache
        in_specs += [pl.BlockSpec((None,) + ck.shape[1:], lambda b, t: (b, 0, 0)),
                     pl.BlockSpec((None,) + cv.shape[1:], lambda b, t: (b, 0, 0, 0))]
        args += [ck, cv]
        scratch = [pltpu.VMEM((cv.shape[1], d_att), _BF16)] * 2
    else:
        scratch = []
    in_specs += [pl.BlockSpec(lam_vecs.shape, lambda b, t: (0, 0)),
                 pl.BlockSpec((1, 2 * qk), lambda b, t: (0, 0))]
    args += [lam_vecs, subln_g.reshape(1, 2 * qk)]
    out_specs = pl.BlockSpec((tq, d_att), lambda b, t: (b * q_tiles + t, 0))
    out_shape = jax.ShapeDtypeStruct((r, d_att), _BF16)
    side_cast = None
    if cast_src is not None:
        side_cast = _SideCast(cast_src, r // seq, q_tiles)
        in_specs.append(side_cast.spec)
        args.append(cast_src)
        out_specs, out_shape = [out_specs, side_cast.out_spec], [out_shape, side_cast.out_shape]
    return pl.pallas_call(
        functools.partial(_attn_kernel, n_heads=n_heads, qk=qk, lam_init=lam_init,
                          cached=cache is not None, side_cast=side_cast),
        grid=(r // seq, q_tiles),
        in_specs=in_specs, out_specs=out_specs, out_shape=out_shape,
        scratch_shapes=scratch,
        compiler_params=_params("parallel", "arbitrary"),
        name="diff_attention_latent" if cache is not None else "diff_attention_context",
    )(*args)


def _outproj_kernel(a_ref, c_ref, wa_ref, wc_ref, x_ref, mod_ref, *rest, gate_idx, side_cast):
    if side_cast is None:
        o_ref, = rest
    else:
        src_ref, o_ref, dst_ref = rest
        side_cast.run(src_ref, dst_ref)
    y = (jnp.dot(a_ref[...], wa_ref[...], preferred_element_type=_F32)
         + jnp.dot(c_ref[...], wc_ref[...], preferred_element_type=_F32))
    o_ref[...] = x_ref[...] + mod_ref[gate_idx:gate_idx + 1, :] * y


def _out_proj(attn, conv, w_out, x, mod3, gate_idx, cast_src=None):
    r, d = x.shape
    d_att, d_conv = attn.shape[1], conv.shape[1]
    assert d_att == d_conv
    rows_per_group = r // mod3.shape[0]
    tm = _tile(rows_per_group, 1024)
    tn = _tile(d, 512 if cast_src is not None else 1024)
    in_specs = [pl.BlockSpec((tm, d_att), lambda i, j: (i, 0)),
                pl.BlockSpec((tm, d_conv), lambda i, j: (i, 0)),
                pl.BlockSpec((d_att, tn), lambda i, j: (0, j)),
                pl.BlockSpec((d_conv, tn), lambda i, j: (1, j)),
                pl.BlockSpec((tm, tn), lambda i, j: (i, j)),
                pl.BlockSpec((None, _N_MOD, tn), lambda i, j: ((i * tm) // rows_per_group, 0, j))]
    args = [attn, conv, w_out, w_out, x, mod3]
    out_specs = pl.BlockSpec((tm, tn), lambda i, j: (i, j))
    out_shape = jax.ShapeDtypeStruct((r, d), _F32)
    side_cast = None
    if cast_src is not None:
        side_cast = _SideCast(cast_src, r // tm, d // tn)
        in_specs.append(side_cast.spec)
        args.append(cast_src)
        out_specs, out_shape = [out_specs, side_cast.out_spec], [out_shape, side_cast.out_shape]
    return pl.pallas_call(
        functools.partial(_outproj_kernel, gate_idx=gate_idx, side_cast=side_cast),
        grid=(r // tm, d // tn),
        in_specs=in_specs, out_specs=out_specs, out_shape=out_shape,
        compiler_params=_params("parallel", "arbitrary"),
        name="out_proj",
    )(*args)


def _mlp_kernel(h_ref, w1_ref, w2_ref, x_ref, mod_ref, o_ref, hid_ref, *, gate_idx, n_f, tf):
    s = pl.program_id(1)

    @pl.when(s < n_f)
    def _():
        hid = jnp.dot(h_ref[...], w1_ref[...], preferred_element_type=_F32)
        col = pl.multiple_of(s * tf, tf)
        hid_ref[:, pl.ds(col, tf)] = jnp.square(jnp.maximum(hid, 0.0)).astype(hid_ref.dtype)

    @pl.when(s >= n_f)
    def _():
        y = jnp.dot(hid_ref[...], w2_ref[...], preferred_element_type=_F32)
        o_ref[...] = x_ref[...] + mod_ref[gate_idx:gate_idx + 1, :] * y


def _mlp(h, w1, w2, x, mod3, gate_idx):
    r, d = x.shape
    d_ff = w1.shape[1]
    rows_per_group = r // mod3.shape[0]
    tm = _tile(rows_per_group, 512)
    tf = _tile(d_ff, 1024)
    tn = _tile(d, 256)
    n_f = d_ff // tf

    def out_col(s):
        return jnp.maximum(s - n_f, 0)

    return pl.pallas_call(
        functools.partial(_mlp_kernel, gate_idx=gate_idx, n_f=n_f, tf=tf),
        grid=(r // tm, n_f + d // tn),
        in_specs=[pl.BlockSpec((tm, d), lambda i, s: (i, 0)),
                  pl.BlockSpec((d, tf), lambda i, s: (0, jnp.minimum(s, n_f - 1))),
                  pl.BlockSpec((d_ff, tn), lambda i, s: (0, out_col(s))),
                  pl.BlockSpec((tm, tn), lambda i, s: (i, out_col(s))),
                  pl.BlockSpec((None, _N_MOD, tn),
                               lambda i, s: ((i * tm) // rows_per_group, 0, out_col(s)))],
        out_specs=pl.BlockSpec((tm, tn), lambda i, s: (i, out_col(s))),
        out_shape=jax.ShapeDtypeStruct((r, d), _F32),
        scratch_shapes=[pltpu.VMEM((tm, d_ff), _BF16)],
        compiler_params=_params("parallel", "arbitrary"),
        name="mlp",
    )(h, w1, w2, x, mod3)


def _rope_tables(n_tokens, qk):
    axis_dim = qk // 2
    rows = n_tokens // _GRID_W
    row = jnp.repeat(jnp.arange(rows, dtype=_F32), _GRID_W)
    col = jnp.tile(jnp.arange(_GRID_W, dtype=_F32), rows)
    inv = jnp.power(_ROPE_THETA, -jnp.arange(0, axis_dim, 2, dtype=_F32) / axis_dim)
    row_ang, col_ang = row[:, None] * inv, col[:, None] * inv
    cos = jnp.concatenate([jnp.cos(row_ang)] * 2 + [jnp.cos(col_ang)] * 2, axis=-1)
    sin = jnp.concatenate([-jnp.sin(row_ang), jnp.sin(row_ang),
                           -jnp.sin(col_ang), jnp.sin(col_ang)], axis=-1)
    return cos, sin


def _trunk(x, mod3, seq, w, n_heads, qk, lam_init, rope=None, cache=None):
    d_att = n_heads * 2 * qk
    context = rope is None
    h = _prenorm(x, w["norm_attn_g"], mod3, 0, 1)
    if context:
        w = dict(w)
        assert w["w_in"].shape[1] == 2 * w["w_qkv"].shape[1]
        qkv, new_k, new_v, w["w_conv"] = _qkv_proj(h, w["w_qkv"], w["q_norm_g"], w["k_norm_g"], d_att, qk,
                                                   side_cast_args=(w["w_in"], 1, 2))
        conv, w["w_mlp_in"] = _conv_proj(h, w["w_conv"], w["conv_w"], seq, cast_src=w["w_mlp_in"])
        attn, w["w_out"] = _attention(qkv, w["lam_vecs"], w["subln_g"], n_heads, qk, seq, lam_init,
                                      cast_src=w["w_out"])
    else:
        qkv = _qkv_proj(h, w["w_qkv"], w["q_norm_g"], w["k_norm_g"], d_att, qk, rope)
        conv = _conv_proj(h, w["w_conv"], w["conv_w"], seq)
        attn = _attention(qkv, w["lam_vecs"], w["subln_g"], n_heads, qk, seq, lam_init, cache)
        new_k = new_v = None
    if context:
        x2, w["w_mlp_out"] = _out_proj(attn, conv, w["w_out"], x, mod3, 2, cast_src=w["w_mlp_out"])
    else:
        x2 = _out_proj(attn, conv, w["w_out"], x, mod3, 2)
    h2 = _prenorm(x2, w["norm_mlp_g"], mod3, 3, 4)
    y = _mlp(h2, w["w_mlp_in"], w["w_mlp_out"], x2, mod3, 5)
    return y, new_k, new_v, w


def kernel(x_prompt, x_sample, cache_k, cache_v, c, c_ctx, w_ada, b_ada, norm_attn_g, w_in, q_norm_g, k_norm_g, lambda_q1, lambda_k1, lambda_q2, lambda_k2, subln_g, conv_w, w_out, norm_mlp_g, w_mlp_in, w_mlp_out):
    assert w_in.shape[0] == 1, "one trunk layer"
    batch, seq, d = x_prompt.shape
    dec_batch, dec_seq, _ = x_sample.shape
    past, n_heads, _, qk = cache_k.shape[2:]
    d_att = n_heads * 2 * qk
    layer = 0
    lam_init = 0.8 - 0.6 * math.exp(-0.3 * layer)

    w = {
        "norm_attn_g": norm_attn_g[layer], "q_norm_g": q_norm_g[layer], "k_norm_g": k_norm_g[layer],
        "subln_g": subln_g[layer], "conv_w": conv_w[layer], "norm_mlp_g": norm_mlp_g[layer],
        "lam_vecs": jnp.stack([lambda_q1[layer], lambda_k1[layer], lambda_q2[layer], lambda_k2[layer]]),
        "w_in": w_in[layer], "w_qkv": w_in[layer][:, :3 * d_att].astype(_BF16), "w_out": w_out[layer],
        "w_mlp_in": w_mlp_in[layer], "w_mlp_out": w_mlp_out[layer],
    }

    n_cond = 1 + dec_batch
    cond = jnp.concatenate([c_ctx[None], c, jnp.zeros((-n_cond % 8, d), _F32)], axis=0)
    mod = _modulation(cond, w_ada[layer], b_ada[layer]).reshape(cond.shape[0], _N_MOD, d)

    y_p, new_k, new_v, w = _trunk(x_prompt.reshape(batch * seq, d), mod[0:1], seq, w,
                                  n_heads, qk, lam_init)
    cache = (cache_k[:, layer].reshape(dec_batch, past * n_heads * 2, qk), cache_v[:, layer])
    x_sample, w["w_mlp_in"] = lax.optimization_barrier((x_sample, w["w_mlp_in"]))
    y_s = _trunk(x_sample.reshape(dec_batch * dec_seq, d), mod[1:n_cond], dec_seq, w,
                 n_heads, qk, lam_init, rope=_rope_tables(dec_seq, qk), cache=cache)[0]

    return (y_p.reshape(batch, seq, d),
            y_s.reshape(dec_batch, dec_seq, d),
            new_k.reshape(batch, 1, seq, n_heads, 2, qk),
            new_v.reshape(batch, 1, seq, n_heads, 2 * qk))
```

```python
import functools
import math

import jax
import jax.numpy as jnp
from jax import lax
from jax.experimental import pallas as pl
from jax.experimental.pallas import tpu as pltpu

_GRID_W = 64
_ROPE_THETA = 10000.0
_EPS = 1e-6
_N_MOD = 6
_LANES = 128
_VMEM_LIMIT = 63 * 1024 * 1024

_F32 = jnp.float32
_BF16 = jnp.bfloat16


def _tile(dim, pref):
    t = min(dim, pref)
    assert dim % t == 0, (dim, pref)
    return t


def _params(*sem):
    return pltpu.CompilerParams(dimension_semantics=sem, vmem_limit_bytes=_VMEM_LIMIT)


class _SideCast:
    def __init__(self, src, n_i, n_j, col_block=0, n_col_blocks=1):
        rows, cols = src.shape[0], src.shape[1] // n_col_blocks
        n_blocks = 1
        while n_blocks * 2 <= n_i * n_j and rows % (n_blocks * 2) == 0 and rows // (n_blocks * 2) >= 16:
            n_blocks *= 2
        self.src, self.n_j, self.n_blocks, self.all_steps = src, n_j, n_blocks, n_blocks == n_i * n_j
        block = (rows // n_blocks, cols)

        def row_block(*idx):
            step = idx[0] * n_j + idx[1] if len(idx) > 1 else idx[0]
            return jnp.minimum(step, n_blocks - 1)

        self.spec = pl.BlockSpec(block, lambda *idx: (row_block(*idx), col_block))
        self.out_spec = pl.BlockSpec(block, lambda *idx: (row_block(*idx), 0))
        self.out_shape = jax.ShapeDtypeStruct((rows, cols), _BF16)

    def run(self, src_ref, dst_ref):
        def cast():
            dst_ref[...] = src_ref[...].astype(dst_ref.dtype)
        if self.all_steps:
            cast()
        else:
            pl.when(pl.program_id(0) * self.n_j + pl.program_id(1) < self.n_blocks)(cast)


def _mod_kernel(cond_ref, w_ref, b_ref, o_ref):
    c = cond_ref[...]
    s = c / (1.0 + jnp.exp(-c))
    o_ref[...] = jnp.dot(s, w_ref[...], preferred_element_type=_F32) + b_ref[...]


def _modulation(cond, w_ada, b_ada):
    r, d = cond.shape
    n = w_ada.shape[1]
    tn = _tile(n, 1024)
    return pl.pallas_call(
        _mod_kernel,
        grid=(n // tn,),
        in_specs=[pl.BlockSpec((r, d), lambda j: (0, 0)),
                  pl.BlockSpec((d, tn), lambda j: (0, j)),
                  pl.BlockSpec((1, tn), lambda j: (0, j))],
        out_specs=pl.BlockSpec((r, tn), lambda j: (0, j)),
        out_shape=jax.ShapeDtypeStruct((r, n), _F32),
        compiler_params=_params("arbitrary"),
        name="modulation",
    )(cond, w_ada, b_ada.reshape(1, n))


def _prenorm_kernel(x_ref, g_ref, mod_ref, o_ref, *, shift_idx, scale_idx):
    x = x_ref[...]
    gain = g_ref[...] * (1.0 + mod_ref[scale_idx:scale_idx + 1, :])
    shift = mod_ref[shift_idx:shift_idx + 1, :]
    r = lax.rsqrt(jnp.mean(x * x, axis=-1, keepdims=True) + _EPS)
    o_ref[...] = (x * r * gain + shift).astype(o_ref.dtype)


def _prenorm(x, g, mod3, shift_idx, scale_idx):
    r, d = x.shape
    rows_per_group = r // mod3.shape[0]
    tr = _tile(rows_per_group, 512)
    return pl.pallas_call(
        functools.partial(_prenorm_kernel, shift_idx=shift_idx, scale_idx=scale_idx),
        grid=(r // tr,),
        in_specs=[pl.BlockSpec((tr, d), lambda i: (i, 0)),
                  pl.BlockSpec((1, d), lambda i: (0, 0)),
                  pl.BlockSpec((None, _N_MOD, d), lambda i: ((i * tr) // rows_per_group, 0, 0))],
        out_specs=pl.BlockSpec((tr, d), lambda i: (i, 0)),
        out_shape=jax.ShapeDtypeStruct((r, d), _BF16),
        compiler_params=_params("parallel"),
        name="prenorm",
    )(x, g.reshape(1, d), mod3)


def _headnorm(xc, g, scale):
    r = lax.rsqrt(jnp.mean(xc * xc, axis=-1, keepdims=True) + _EPS)
    if scale != 1.0:
        r = r * scale
    return xc * r * g


def _rope(y, cos, sin_signed, first_half):
    qk = y.shape[-1]
    ahead = pltpu.roll(y, qk - qk // 4, axis=1)
    behind = pltpu.roll(y, qk // 4, axis=1)
    return y * cos + jnp.where(first_half, ahead, behind) * sin_signed


def _qkv_kernel(*refs, latent, qk, n_sec, q_scale, side_cast):
    if latent:
        x_ref, w_ref, qg_ref, kg_ref, cos_ref, sin_ref, qkv_ref = refs
    else:
        x_ref, w_ref, qg_ref, kg_ref, src_ref, qkv_ref, newk_ref, newv_ref, dst_ref = refs
        side_cast.run(src_ref, dst_ref)
    j = pl.program_id(1)
    tn = w_ref.shape[1]

    def project():
        return jnp.dot(x_ref[...], w_ref[...], preferred_element_type=_F32)

    if latent:
        shared = project()
        project = lambda: shared

    def qk_epilogue(g_ref, scale, f32_ref):
        acc = project()
        g = g_ref[...]
        if latent:
            cos = cos_ref[...]
            sin = sin_ref[...]
            lane = lax.broadcasted_iota(jnp.int32, (1, qk), 1)
            first_half = (lane % (qk // 2)) < (qk // 4)
        for c in range(tn // qk):
            cols = slice(c * qk, (c + 1) * qk)
            y = _headnorm(acc[:, cols], g, scale)
            if latent:
                y = _rope(y, cos, sin, first_half)
            if f32_ref is not None:
                n_chunks = n_sec * (tn // qk)
                chunk = (j - n_sec) * (tn // qk) + c
                f32_ref[pl.ds(chunk, acc.shape[0], stride=n_chunks), :] = y
            qkv_ref[:, cols] = y.astype(qkv_ref.dtype)

    @pl.when(j < n_sec)
    def _():
        qk_epilogue(qg_ref, q_scale, None)

    @pl.when((j >= n_sec) & (j < 2 * n_sec))
    def _():
        qk_epilogue(kg_ref, 1.0, None if latent else newk_ref)

    @pl.when(j >= 2 * n_sec)
    def _():
        acc = project()
        qkv_ref[...] = acc.astype(qkv_ref.dtype)
        if not latent:
            newv_ref[...] = acc


def _qkv_proj(h, w_qkv, q_g, k_g, d_att, qk, rope=None, side_cast_args=None):
    r, d = h.shape
    latent = rope is not None
    tm = _tile(r if not latent else rope[0].shape[0], 1024)
    tn = _tile(d_att, 512)
    n_sec = d_att // tn
    in_specs = [pl.BlockSpec((tm, d), lambda i, j: (i, 0)),
                pl.BlockSpec((d, tn), lambda i, j: (0, j)),
                pl.BlockSpec((1, qk), lambda i, j: (0, 0)),
                pl.BlockSpec((1, qk), lambda i, j: (0, 0))]
    args = [h, w_qkv, q_g.reshape(1, qk), k_g.reshape(1, qk)]
    qkv_spec = pl.BlockSpec((tm, tn), lambda i, j: (i, j))
    qkv_shape = jax.ShapeDtypeStruct((r, 3 * d_att), _BF16)
    if latent:
        seq_tiles = rope[0].shape[0] // tm
        rope_spec = pl.BlockSpec((tm, qk), lambda i, j: (i % seq_tiles, 0))
        in_specs += [rope_spec, rope_spec]
        args += list(rope)
        out_specs, out_shape = qkv_spec, qkv_shape
        side_cast = None
    else:
        side_cast = _SideCast(side_cast_args[0], r // tm, 3 * n_sec, *side_cast_args[1:])
        in_specs.append(side_cast.spec)
        args.append(side_cast.src)
        n_chunks = d_att // qk
        out_specs = [qkv_spec,
                     pl.BlockSpec((tm * n_chunks, qk), lambda i, j: (i, 0)),
                     pl.BlockSpec((tm, tn), lambda i, j: (i, jnp.clip(j - 2 * n_sec, 0, n_sec - 1))),
                     side_cast.out_spec]
        out_shape = [qkv_shape,
                     jax.ShapeDtypeStruct((r * n_chunks, qk), _F32),
                     jax.ShapeDtypeStruct((r, d_att), _F32),
                     side_cast.out_shape]
    return pl.pallas_call(
        functools.partial(_qkv_kernel, latent=latent, qk=qk, n_sec=n_sec, q_scale=qk ** -0.5,
                          side_cast=side_cast),
        grid=(r // tm, 3 * n_sec),
        in_specs=in_specs, out_specs=out_specs, out_shape=out_shape,
        compiler_params=_params("parallel", "arbitrary"),
        name="qkv_proj_latent" if latent else "qkv_proj_context",
    )(*args)


def _conv_kernel(x_ref, wb_ref, wc_ref, wu_ref, cw_ref, *rest, seq, side_cast):
    if side_cast is None:
        o_ref, = rest
    else:
        src_ref, o_ref, dst_ref = rest
        side_cast.run(src_ref, dst_ref)
    x = x_ref[...]
    b = jnp.dot(x, wb_ref[...], preferred_element_type=_F32)
    c = jnp.dot(x, wc_ref[...], preferred_element_type=_F32)
    u = jnp.dot(x, wu_ref[...], preferred_element_type=_F32)
    z = c * u
    tm = z.shape[0]
    pos = lax.broadcasted_iota(jnp.int32, (tm, 1), 0) % seq
    z_prev = jnp.where(pos == 0, 0.0, pltpu.roll(z, 1, axis=0))
    z_next = jnp.where(pos == seq - 1, 0.0, pltpu.roll(z, tm - 1, axis=0))
    cw = cw_ref[...]
    y = cw[0:1, :] * z_prev + cw[1:2, :] * z + cw[2:3, :] * z_next
    o_ref[...] = (b * y).astype(o_ref.dtype)


def _conv_proj(h, w_conv, conv_w, seq, cast_src=None):
    r, d = h.shape
    d_conv = conv_w.shape[1]
    tm = _tile(r, max(seq, 1024))
    assert tm % seq == 0
    tc = _tile(d_conv, 256 if cast_src is not None else 512)
    nc = d_conv // tc

    def w_spec(sec):
        return pl.BlockSpec((d, tc), lambda i, j: (0, sec * nc + j))

    in_specs = [pl.BlockSpec((tm, d), lambda i, j: (i, 0)),
                w_spec(0), w_spec(1), w_spec(2),
                pl.BlockSpec((conv_w.shape[0], tc), lambda i, j: (0, j))]
    args = [h, w_conv, w_conv, w_conv, conv_w]
    out_specs = pl.BlockSpec((tm, tc), lambda i, j: (i, j))
    out_shape = jax.ShapeDtypeStruct((r, d_conv), _BF16)
    side_cast = None
    if cast_src is not None:
        side_cast = _SideCast(cast_src, r // tm, nc)
        in_specs.append(side_cast.spec)
        args.append(cast_src)
        out_specs, out_shape = [out_specs, side_cast.out_spec], [out_shape, side_cast.out_shape]
    return pl.pallas_call(
        functools.partial(_conv_kernel, seq=seq, side_cast=side_cast),
        grid=(r // tm, nc),
        in_specs=in_specs, out_specs=out_specs, out_shape=out_shape,
        compiler_params=_params("parallel", "arbitrary"),
        name="conv_proj",
    )(*args)


def _attn_kernel(*refs, n_heads, qk, lam_init, cached, side_cast):
    hw = 2 * qk
    if cached:
        q_ref, k_ref, v_ref, ck_ref, cv_ref, lam_ref, g_ref, o_ref, ckb_ref, cvb_ref = refs
        past = cv_ref.shape[0]

        @pl.when(pl.program_id(1) == 0)
        def _():
            for h in range(n_heads):
                for c in range(2):
                    rows = pl.ds(2 * h + c, past, stride=2 * n_heads)
                    ckb_ref[:, h * hw + c * qk:h * hw + (c + 1) * qk] = ck_ref[rows, :].astype(_BF16)
                cvb_ref[:, h * hw:(h + 1) * hw] = cv_ref[:, h, :].astype(_BF16)
    elif side_cast is not None:
        q_ref, k_ref, v_ref, lam_ref, g_ref, src_ref, o_ref, dst_ref = refs
        side_cast.run(src_ref, dst_ref)
    else:
        q_ref, k_ref, v_ref, lam_ref, g_ref, o_ref = refs
    lv = lam_ref[...]
    lam = (jnp.exp(jnp.sum(lv[0:1] * lv[1:2], axis=-1, keepdims=True))
           - jnp.exp(jnp.sum(lv[2:3] * lv[3:4], axis=-1, keepdims=True)) + lam_init)
    g = g_ref[...]
    dims = (((1,), (1,)), ((), ()))

    def probs(q, keys):
        s = [lax.dot_general(q, k, dims, preferred_element_type=_F32) for k in keys]
        m = functools.reduce(jnp.maximum, [jnp.max(x, axis=-1, keepdims=True) for x in s])
        e = [jnp.exp(x - m) for x in s]
        den = functools.reduce(jnp.add, [jnp.sum(x, axis=-1, keepdims=True) for x in e])
        return e, 1.0 / den

    for h in range(n_heads):
        o = None
        c0, c1 = h * hw, h * hw + qk
        keys0 = [k_ref[:, c0:c0 + qk]]
        keys1 = [k_ref[:, c1:c1 + qk]]
        vals = [v_ref[:, c0:c0 + hw]]
        if cached:
            keys0.append(ckb_ref[:, c0:c0 + qk])
            keys1.append(ckb_ref[:, c1:c1 + qk])
            vals.append(cvb_ref[:, c0:c0 + hw])
        e0, r0 = probs(q_ref[:, c0:c0 + qk], keys0)
        e1, r1 = probs(q_ref[:, c1:c1 + qk], keys1)
        r1 = r1 * lam
        for p0, p1, v in zip(e0, e1, vals):
            a = (p0 * r0 - p1 * r1).astype(_BF16)
            part = jnp.dot(a, v, preferred_element_type=_F32)
            o = part if o is None else o + part
        o = o * lax.rsqrt(jnp.mean(o * o, axis=-1, keepdims=True) + _EPS) * g * (1.0 - lam_init)
        o_ref[:, c0:c0 + hw] = o.astype(o_ref.dtype)


def _attention(qkv, lam_vecs, subln_g, n_heads, qk, seq, lam_init, cache=None, cast_src=None):
    r = qkv.shape[0]
    d_att = n_heads * 2 * qk
    tq = _tile(seq, 256)
    q_tiles = seq // tq
    in_specs = [pl.BlockSpec((tq, d_att), lambda b, t: (b * q_tiles + t, 0)),
                pl.BlockSpec((seq, d_att), lambda b, t: (b, 1)),
                pl.BlockSpec((seq, d_att), lambda b, t: (b, 2))]
    args = [qkv, qkv, qkv]
    if cache is not None:
        ck, cv = cache
        in_specs += [pl.BlockSpec((None,) + ck.shape[1:], lambda b, t: (b, 0, 0)),
                     pl.BlockSpec((None,) + cv.shape[1:], lambda b, t: (b, 0, 0, 0))]
        args += [ck, cv]
        scratch = [pltpu.VMEM((cv.shape[1], d_att), _BF16)] * 2
    else:
        scratch = []
    in_specs += [pl.BlockSpec(lam_vecs.shape, lambda b, t: (0, 0)),
                 pl.BlockSpec((1, 2 * qk), lambda b, t: (0, 0))]
    args += [lam_vecs, subln_g.reshape(1, 2 * qk)]
    out_specs = pl.BlockSpec((tq, d_att), lambda b, t: (b * q_tiles + t, 0))
    out_shape = jax.ShapeDtypeStruct((r, d_att), _BF16)
    side_cast = None
    if cast_src is not None:
        side_cast = _SideCast(cast_src, r // seq, q_tiles)
        in_specs.append(side_cast.spec)
        args.append(cast_src)
        out_specs, out_shape = [out_specs, side_cast.out_spec], [out_shape, side_cast.out_shape]
    return pl.pallas_call(
        functools.partial(_attn_kernel, n_heads=n_heads, qk=qk, lam_init=lam_init,
                          cached=cache is not None, side_cast=side_cast),
        grid=(r // seq, q_tiles),
        in_specs=in_specs, out_specs=out_specs, out_shape=out_shape,
        scratch_shapes=scratch,
        compiler_params=_params("parallel", "arbitrary"),
        name="diff_attention_latent" if cache is not None else "diff_attention_context",
    )(*args)


def _outproj_kernel(a_ref, c_ref, wa_ref, wc_ref, x_ref, mod_ref, *rest, gate_idx, side_cast):
    if side_cast is None:
        o_ref, = rest
    else:
        src_ref, o_ref, dst_ref = rest
        side_cast.run(src_ref, dst_ref)
    y = (jnp.dot(a_ref[...], wa_ref[...], preferred_element_type=_F32)
         + jnp.dot(c_ref[...], wc_ref[...], preferred_element_type=_F32))
    o_ref[...] = x_ref[...] + mod_ref[gate_idx:gate_idx + 1, :] * y


def _out_proj(attn, conv, w_out, x, mod3, gate_idx, cast_src=None):
    r, d = x.shape
    d_att, d_conv = attn.shape[1], conv.shape[1]
    assert d_att == d_conv
    rows_per_group = r // mod3.shape[0]
    tm = _tile(rows_per_group, 1024)
    tn = _tile(d, 512 if cast_src is not None else 1024)
    in_specs = [pl.BlockSpec((tm, d_att), lambda i, j: (i, 0)),
                pl.BlockSpec((tm, d_conv), lambda i, j: (i, 0)),
                pl.BlockSpec((d_att, tn), lambda i, j: (0, j)),
                pl.BlockSpec((d_conv, tn), lambda i, j: (1, j)),
                pl.BlockSpec((tm, tn), lambda i, j: (i, j)),
                pl.BlockSpec((None, _N_MOD, tn), lambda i, j: ((i * tm) // rows_per_group, 0, j))]
    args = [attn, conv, w_out, w_out, x, mod3]
    out_specs = pl.BlockSpec((tm, tn), lambda i, j: (i, j))
    out_shape = jax.ShapeDtypeStruct((r, d), _F32)
    side_cast = None
    if cast_src is not None:
        side_cast = _SideCast(cast_src, r // tm, d // tn)
        in_specs.append(side_cast.spec)
        args.append(cast_src)
        out_specs, out_shape = [out_specs, side_cast.out_spec], [out_shape, side_cast.out_shape]
    return pl.pallas_call(
        functools.partial(_outproj_kernel, gate_idx=gate_idx, side_cast=side_cast),
        grid=(r // tm, d // tn),
        in_specs=in_specs, out_specs=out_specs, out_shape=out_shape,
        compiler_params=_params("parallel", "arbitrary"),
        name="out_proj",
    )(*args)


def _mlp_kernel(h_ref, w1_ref, w2_ref, x_ref, mod_ref, o_ref, hid_ref, *, gate_idx, n_f, tf):
    s = pl.program_id(1)

    @pl.when(s < n_f)
    def _():
        hid = jnp.dot(h_ref[...], w1_ref[...], preferred_element_type=_F32)
        col = pl.multiple_of(s * tf, tf)
        hid_ref[:, pl.ds(col, tf)] = jnp.square(jnp.maximum(hid, 0.0)).astype(hid_ref.dtype)

    @pl.when(s >= n_f)
    def _():
        y = jnp.dot(hid_ref[...], w2_ref[...], preferred_element_type=_F32)
        o_ref[...] = x_ref[...] + mod_ref[gate_idx:gate_idx + 1, :] * y


def _mlp(h, w1, w2, x, mod3, gate_idx):
    r, d = x.shape
    d_ff = w1.shape[1]
    rows_per_group = r // mod3.shape[0]
    tm = _tile(rows_per_group, 512)
    tf = _tile(d_ff, 1024)
    tn = _tile(d, 256)
    n_f = d_ff // tf

    def out_col(s):
        return jnp.maximum(s - n_f, 0)

    return pl.pallas_call(
        functools.partial(_mlp_kernel, gate_idx=gate_idx, n_f=n_f, tf=tf),
        grid=(r // tm, n_f + d // tn),
        in_specs=[pl.BlockSpec((tm, d), lambda i, s: (i, 0)),
                  pl.BlockSpec((d, tf), lambda i, s: (0, jnp.minimum(s, n_f - 1))),
                  pl.BlockSpec((d_ff, tn), lambda i, s: (0, out_col(s))),
                  pl.BlockSpec((tm, tn), lambda i, s: (i, out_col(s))),
                  pl.BlockSpec((None, _N_MOD, tn),
                               lambda i, s: ((i * tm) // rows_per_group, 0, out_col(s)))],
        out_specs=pl.BlockSpec((tm, tn), lambda i, s: (i, out_col(s))),
        out_shape=jax.ShapeDtypeStruct((r, d), _F32),
        scratch_shapes=[pltpu.VMEM((tm, d_ff), _BF16)],
        compiler_params=_params("parallel", "arbitrary"),
        name="mlp",
    )(h, w1, w2, x, mod3)


def _rope_tables(n_tokens, qk):
    axis_dim = qk // 2
    rows = n_tokens // _GRID_W
    row = jnp.repeat(jnp.arange(rows, dtype=_F32), _GRID_W)
    col = jnp.tile(jnp.arange(_GRID_W, dtype=_F32), rows)
    inv = jnp.power(_ROPE_THETA, -jnp.arange(0, axis_dim, 2, dtype=_F32) / axis_dim)
    row_ang, col_ang = row[:, None] * inv, col[:, None] * inv
    cos = jnp.concatenate([jnp.cos(row_ang)] * 2 + [jnp.cos(col_ang)] * 2, axis=-1)
    sin = jnp.concatenate([-jnp.sin(row_ang), jnp.sin(row_ang),
                           -jnp.sin(col_ang), jnp.sin(col_ang)], axis=-1)
    return cos, sin


def _trunk(x, mod3, seq, w, n_heads, qk, lam_init, rope=None, cache=None):
    d_att = n_heads * 2 * qk
    context = rope is None
    h = _prenorm(x, w["norm_attn_g"], mod3, 0, 1)
    if context:
        w = dict(w)
        assert w["w_in"].shape[1] == 2 * w["w_qkv"].shape[1]
        qkv, new_k, new_v, w["w_conv"] = _qkv_proj(h, w["w_qkv"], w["q_norm_g"], w["k_norm_g"], d_att, qk,
                                                   side_cast_args=(w["w_in"], 1, 2))
        attn, w["w_out"] = _attention(qkv, w["lam_vecs"], w["subln_g"], n_heads, qk, seq, lam_init,
                                      cast_src=w["w_out"])
        h, attn = lax.optimization_barrier((h, attn))
        conv, w["w_mlp_in"] = _conv_proj(h, w["w_conv"], w["conv_w"], seq, cast_src=w["w_mlp_in"])
    else:
        qkv = _qkv_proj(h, w["w_qkv"], w["q_norm_g"], w["k_norm_g"], d_att, qk, rope)
        conv = _conv_proj(h, w["w_conv"], w["conv_w"], seq)
        attn = _attention(qkv, w["lam_vecs"], w["subln_g"], n_heads, qk, seq, lam_init, cache)
        new_k = new_v = None
    if context:
        x2, w["w_mlp_out"] = _out_proj(attn, conv, w["w_out"], x, mod3, 2, cast_src=w["w_mlp_out"])
    else:
        x2 = _out_proj(attn, conv, w["w_out"], x, mod3, 2)
    h2 = _prenorm(x2, w["norm_mlp_g"], mod3, 3, 4)
    y = _mlp(h2, w["w_mlp_in"], w["w_mlp_out"], x2, mod3, 5)
    return y, new_k, new_v, w


def kernel(x_prompt, x_sample, cache_k, cache_v, c, c_ctx, w_ada, b_ada, norm_attn_g, w_in, q_norm_g, k_norm_g, lambda_q1, lambda_k1, lambda_q2, lambda_k2, subln_g, conv_w, w_out, norm_mlp_g, w_mlp_in, w_mlp_out):
    assert w_in.shape[0] == 1, "one trunk layer"
    batch, seq, d = x_prompt.shape
    dec_batch, dec_seq, _ = x_sample.shape
    past, n_heads, _, qk = cache_k.shape[2:]
    d_att = n_heads * 2 * qk
    layer = 0
    lam_init = 0.8 - 0.6 * math.exp(-0.3 * layer)

    w = {
        "norm_attn_g": norm_attn_g[layer], "q_norm_g": q_norm_g[layer], "k_norm_g": k_norm_g[layer],
        "subln_g": subln_g[layer], "conv_w": conv_w[layer], "norm_mlp_g": norm_mlp_g[layer],
        "lam_vecs": jnp.stack([lambda_q1[layer], lambda_k1[layer], lambda_q2[layer], lambda_k2[layer]]),
        "w_in": w_in[layer], "w_qkv": w_in[layer][:, :3 * d_att].astype(_BF16), "w_out": w_out[layer],
        "w_mlp_in": w_mlp_in[layer], "w_mlp_out": w_mlp_out[layer],
    }

    n_cond = 1 + dec_batch
    cond = jnp.concatenate([c_ctx[None], c, jnp.zeros((-n_cond % 8, d), _F32)], axis=0)
    mod = _modulation(cond, w_ada[layer], b_ada[layer]).reshape(cond.shape[0], _N_MOD, d)

    y_p, new_k, new_v, w = _trunk(x_prompt.reshape(batch * seq, d), mod[0:1], seq, w,
                                  n_heads, qk, lam_init)
    cache = (cache_k[:, layer].reshape(dec_batch, past * n_heads * 2, qk), cache_v[:, layer])
    x_sample, w["w_mlp_in"] = lax.optimization_barrier((x_sample, w["w_mlp_in"]))
    y_s = _trunk(x_sample.reshape(dec_batch * dec_seq, d), mod[1:n_cond], dec_seq, w,
                 n_heads, qk, lam_init, rope=_rope_tables(dec_seq, qk), cache=cache)[0]

    return (y_p.reshape(batch, seq, d),
            y_s.reshape(dec_batch, dec_seq, d),
            new_k.reshape(batch, 1, seq, n_heads, 2, qk),
            new_v.reshape(batch, 1, seq, n_heads, 2 * qk))
```

```python
import functools
import math

import jax
import jax.numpy as jnp
from jax import lax
from jax.experimental import pallas as pl
from jax.experimental.pallas import tpu as pltpu

_GRID_W = 64
_ROPE_THETA = 10000.0
_EPS = 1e-6
_N_MOD = 6
_LANES = 128
_VMEM_LIMIT = 63 * 1024 * 1024

_F32 = jnp.float32
_BF16 = jnp.bfloat16


def _tile(dim, pref):
    t = min(dim, pref)
    assert dim % t == 0, (dim, pref)
    return t


def _params(*sem):
    return pltpu.CompilerParams(dimension_semantics=sem, vmem_limit_bytes=_VMEM_LIMIT)


class _SideCast:
    def __init__(self, src, n_i, n_j, col_block=0, n_col_blocks=1, row_part=0, n_row_parts=1):
        rows, cols = src.shape[0] // n_row_parts, src.shape[1] // n_col_blocks
        n_blocks = 1
        while n_blocks * 2 <= n_i * n_j and rows % (n_blocks * 2) == 0 and rows // (n_blocks * 2) >= 16:
            n_blocks *= 2
        self.src, self.n_j, self.n_blocks, self.all_steps = src, n_j, n_blocks, n_blocks == n_i * n_j
        block = (rows // n_blocks, cols)

        def row_block(*idx):
            step = idx[0] * n_j + idx[1] if len(idx) > 1 else idx[0]
            return jnp.minimum(step, n_blocks - 1)

        self.spec = pl.BlockSpec(block, lambda *idx: (row_part * n_blocks + row_block(*idx), col_block))
        self.out_spec = pl.BlockSpec(block, lambda *idx: (row_block(*idx), 0))
        self.out_shape = jax.ShapeDtypeStruct((rows, cols), _BF16)

    def run(self, src_ref, dst_ref):
        def cast():
            dst_ref[...] = src_ref[...].astype(dst_ref.dtype)
        if self.all_steps:
            cast()
        else:
            pl.when(pl.program_id(0) * self.n_j + pl.program_id(1) < self.n_blocks)(cast)


def _mod_kernel(cond_ref, w_ref, b_ref, o_ref):
    c = cond_ref[...]
    s = c / (1.0 + jnp.exp(-c))
    o_ref[...] = jnp.dot(s, w_ref[...], preferred_element_type=_F32) + b_ref[...]


def _modulation(cond, w_ada, b_ada):
    r, d = cond.shape
    n = w_ada.shape[1]
    tn = _tile(n, 512)
    return pl.pallas_call(
        _mod_kernel,
        grid=(n // tn,),
        in_specs=[pl.BlockSpec((r, d), lambda j: (0, 0)),
                  pl.BlockSpec((d, tn), lambda j: (0, j)),
                  pl.BlockSpec((1, tn), lambda j: (0, j))],
        out_specs=pl.BlockSpec((r, tn), lambda j: (0, j)),
        out_shape=jax.ShapeDtypeStruct((r, n), _F32),
        compiler_params=_params("arbitrary"),
        name="modulation",
    )(cond, w_ada, b_ada.reshape(1, n))


def _prenorm_kernel(x_ref, g_ref, mod_ref, o_ref, *, shift_idx, scale_idx):
    x = x_ref[...]
    gain = g_ref[...] * (1.0 + mod_ref[scale_idx:scale_idx + 1, :])
    shift = mod_ref[shift_idx:shift_idx + 1, :]
    r = lax.rsqrt(jnp.mean(x * x, axis=-1, keepdims=True) + _EPS)
    o_ref[...] = (x * r * gain + shift).astype(o_ref.dtype)


def _prenorm(x, g, mod3, shift_idx, scale_idx):
    r, d = x.shape
    rows_per_group = r // mod3.shape[0]
    tr = _tile(rows_per_group, 512)
    return pl.pallas_call(
        functools.partial(_prenorm_kernel, shift_idx=shift_idx, scale_idx=scale_idx),
        grid=(r // tr,),
        in_specs=[pl.BlockSpec((tr, d), lambda i: (i, 0)),
                  pl.BlockSpec((1, d), lambda i: (0, 0)),
                  pl.BlockSpec((None, _N_MOD, d), lambda i: ((i * tr) // rows_per_group, 0, 0))],
        out_specs=pl.BlockSpec((tr, d), lambda i: (i, 0)),
        out_shape=jax.ShapeDtypeStruct((r, d), _BF16),
        compiler_params=_params("parallel"),
        name="prenorm",
    )(x, g.reshape(1, d), mod3)


def _headnorm(xc, g, scale):
    r = lax.rsqrt(jnp.mean(xc * xc, axis=-1, keepdims=True) + _EPS)
    if scale != 1.0:
        r = r * scale
    return xc * r * g


def _rope(y, cos, sin_signed, first_half):
    qk = y.shape[-1]
    ahead = pltpu.roll(y, qk - qk // 4, axis=1)
    behind = pltpu.roll(y, qk // 4, axis=1)
    return y * cos + jnp.where(first_half, ahead, behind) * sin_signed


def _qkv_kernel(*refs, latent, qk, n_sec, q_scale, side_cast):
    if latent:
        x_ref, w_ref, qg_ref, kg_ref, cos_ref, sin_ref, qkv_ref = refs
    else:
        n_cast = len(side_cast)
        x_ref, w_ref, qg_ref, kg_ref = refs[:4]
        qkv_ref, newk_ref, newv_ref = refs[4 + n_cast:7 + n_cast]
        for cast, src_ref, dst_ref in zip(side_cast, refs[4:4 + n_cast], refs[7 + n_cast:]):
            cast.run(src_ref, dst_ref)
    j = pl.program_id(1)
    tn = w_ref.shape[1]

    def project():
        return jnp.dot(x_ref[...], w_ref[...], preferred_element_type=_F32)

    if latent:
        shared = project()
        project = lambda: shared

    def qk_epilogue(g_ref, scale, f32_ref):
        acc = project()
        g = g_ref[...]
        if latent:
            cos = cos_ref[...]
            sin = sin_ref[...]
            lane = lax.broadcasted_iota(jnp.int32, (1, qk), 1)
            first_half = (lane % (qk // 2)) < (qk // 4)
        for c in range(tn // qk):
            cols = slice(c * qk, (c + 1) * qk)
            y = _headnorm(acc[:, cols], g, scale)
            if latent:
                y = _rope(y, cos, sin, first_half)
            if f32_ref is not None:
                n_chunks = n_sec * (tn // qk)
                chunk = (j - n_sec) * (tn // qk) + c
                f32_ref[pl.ds(chunk, acc.shape[0], stride=n_chunks), :] = y
            qkv_ref[:, cols] = y.astype(qkv_ref.dtype)

    @pl.when(j < n_sec)
    def _():
        qk_epilogue(qg_ref, q_scale, None)

    @pl.when((j >= n_sec) & (j < 2 * n_sec))
    def _():
        qk_epilogue(kg_ref, 1.0, None if latent else newk_ref)

    @pl.when(j >= 2 * n_sec)
    def _():
        acc = project()
        qkv_ref[...] = acc.astype(qkv_ref.dtype)
        if not latent:
            newv_ref[...] = acc


def _qkv_proj(h, w_qkv, q_g, k_g, d_att, qk, rope=None, side_cast_args=None):
    r, d = h.shape
    latent = rope is not None
    tm = _tile(r if not latent else rope[0].shape[0], 1024)
    tn = _tile(d_att, 512)
    n_sec = d_att // tn
    in_specs = [pl.BlockSpec((tm, d), lambda i, j: (i, 0)),
                pl.BlockSpec((d, tn), lambda i, j: (0, j)),
                pl.BlockSpec((1, qk), lambda i, j: (0, 0)),
                pl.BlockSpec((1, qk), lambda i, j: (0, 0))]
    args = [h, w_qkv, q_g.reshape(1, qk), k_g.reshape(1, qk)]
    qkv_spec = pl.BlockSpec((tm, tn), lambda i, j: (i, j))
    qkv_shape = jax.ShapeDtypeStruct((r, 3 * d_att), _BF16)
    if latent:
        seq_tiles = rope[0].shape[0] // tm
        rope_spec = pl.BlockSpec((tm, qk), lambda i, j: (i % seq_tiles, 0))
        in_specs += [rope_spec, rope_spec]
        args += list(rope)
        out_specs, out_shape = qkv_spec, qkv_shape
        side_cast = None
    else:
        side_cast = [_SideCast(a[0], r // tm, 3 * n_sec, *a[1:]) for a in side_cast_args]
        in_specs += [c.spec for c in side_cast]
        args += [c.src for c in side_cast]
        n_chunks = d_att // qk
        out_specs = [qkv_spec,
                     pl.BlockSpec((tm * n_chunks, qk), lambda i, j: (i, 0)),
                     pl.BlockSpec((tm, tn), lambda i, j: (i, jnp.clip(j - 2 * n_sec, 0, n_sec - 1)))]
        out_specs += [c.out_spec for c in side_cast]
        out_shape = [qkv_shape,
                     jax.ShapeDtypeStruct((r * n_chunks, qk), _F32),
                     jax.ShapeDtypeStruct((r, d_att), _F32)]
        out_shape += [c.out_shape for c in side_cast]
    return pl.pallas_call(
        functools.partial(_qkv_kernel, latent=latent, qk=qk, n_sec=n_sec, q_scale=qk ** -0.5,
                          side_cast=side_cast),
        grid=(r // tm, 3 * n_sec),
        in_specs=in_specs, out_specs=out_specs, out_shape=out_shape,
        compiler_params=_params("parallel", "arbitrary"),
        name="qkv_proj_latent" if latent else "qkv_proj_context",
    )(*args)


def _conv_kernel(x_ref, wb_ref, wc_ref, wu_ref, cw_ref, *rest, seq, side_cast):
    if side_cast is None:
        o_ref, = rest
    else:
        src_ref, o_ref, dst_ref = rest
        side_cast.run(src_ref, dst_ref)
    x = x_ref[...]
    b = jnp.dot(x, wb_ref[...], preferred_element_type=_F32)
    c = jnp.dot(x, wc_ref[...], preferred_element_type=_F32)
    u = jnp.dot(x, wu_ref[...], preferred_element_type=_F32)
    z = c * u
    tm = z.shape[0]
    pos = lax.broadcasted_iota(jnp.int32, (tm, 1), 0) % seq
    z_prev = jnp.where(pos == 0, 0.0, pltpu.roll(z, 1, axis=0))
    z_next = jnp.where(pos == seq - 1, 0.0, pltpu.roll(z, tm - 1, axis=0))
    cw = cw_ref[...]
    y = cw[0:1, :] * z_prev + cw[1:2, :] * z + cw[2:3, :] * z_next
    o_ref[...] = (b * y).astype(o_ref.dtype)


def _conv_proj(h, w_conv, conv_w, seq, cast_src=None):
    r, d = h.shape
    d_conv = conv_w.shape[1]
    tm = _tile(r, max(seq, 1024))
    assert tm % seq == 0
    tc = _tile(d_conv, 256 if cast_src is not None else 512)
    nc = d_conv // tc

    def w_spec(sec):
        return pl.BlockSpec((d, tc), lambda i, j: (0, sec * nc + j))

    in_specs = [pl.BlockSpec((tm, d), lambda i, j: (i, 0)),
                w_spec(0), w_spec(1), w_spec(2),
                pl.BlockSpec((conv_w.shape[0], tc), lambda i, j: (0, j))]
    args = [h, w_conv, w_conv, w_conv, conv_w]
    out_specs = pl.BlockSpec((tm, tc), lambda i, j: (i, j))
    out_shape = jax.ShapeDtypeStruct((r, d_conv), _BF16)
    side_cast = None
    if cast_src is not None:
        side_cast = _SideCast(cast_src, r // tm, nc)
        in_specs.append(side_cast.spec)
        args.append(cast_src)
        out_specs, out_shape = [out_specs, side_cast.out_spec], [out_shape, side_cast.out_shape]
    return pl.pallas_call(
        functools.partial(_conv_kernel, seq=seq, side_cast=side_cast),
        grid=(r // tm, nc),
        in_specs=in_specs, out_specs=out_specs, out_shape=out_shape,
        compiler_params=_params("parallel", "arbitrary"),
        name="conv_proj",
    )(*args)


def _attn_kernel(*refs, n_heads, qk, lam_init, cached, side_cast):
    hw = 2 * qk
    if cached:
        q_ref, k_ref, v_ref, ck_ref, cv_ref, lam_ref, g_ref, o_ref, ckb_ref, cvb_ref = refs
        past = cv_ref.shape[0]

        @pl.when(pl.program_id(1) == 0)
        def _():
            for h in range(n_heads):
                for c in range(2):
                    rows = pl.ds(2 * h + c, past, stride=2 * n_heads)
                    ckb_ref[:, h * hw + c * qk:h * hw + (c + 1) * qk] = ck_ref[rows, :].astype(_BF16)
                cvb_ref[:, h * hw:(h + 1) * hw] = cv_ref[:, h, :].astype(_BF16)
    elif side_cast is not None:
        q_ref, k_ref, v_ref, lam_ref, g_ref, src_ref, o_ref, dst_ref = refs
        side_cast.run(src_ref, dst_ref)
    else:
        q_ref, k_ref, v_ref, lam_ref, g_ref, o_ref = refs
    lv = lam_ref[...]
    lam = (jnp.exp(jnp.sum(lv[0:1] * lv[1:2], axis=-1, keepdims=True))
           - jnp.exp(jnp.sum(lv[2:3] * lv[3:4], axis=-1, keepdims=True)) + lam_init)
    g = g_ref[...]
    dims = (((1,), (1,)), ((), ()))

    def probs(q, keys):
        s = [lax.dot_general(q, k, dims, preferred_element_type=_F32) for k in keys]
        m = functools.reduce(jnp.maximum, [jnp.max(x, axis=-1, keepdims=True) for x in s])
        e = [jnp.exp(x - m) for x in s]
        den = functools.reduce(jnp.add, [jnp.sum(x, axis=-1, keepdims=True) for x in e])
        return e, 1.0 / den

    for h in range(n_heads):
        o = None
        c0, c1 = h * hw, h * hw + qk
        keys0 = [k_ref[:, c0:c0 + qk]]
        keys1 = [k_ref[:, c1:c1 + qk]]
        vals = [v_ref[:, c0:c0 + hw]]
        if cached:
            keys0.append(ckb_ref[:, c0:c0 + qk])
            keys1.append(ckb_ref[:, c1:c1 + qk])
            vals.append(cvb_ref[:, c0:c0 + hw])
        e0, r0 = probs(q_ref[:, c0:c0 + qk], keys0)
        e1, r1 = probs(q_ref[:, c1:c1 + qk], keys1)
        r1 = r1 * lam
        for p0, p1, v in zip(e0, e1, vals):
            a = (p0 * r0 - p1 * r1).astype(_BF16)
            part = jnp.dot(a, v, preferred_element_type=_F32)
            o = part if o is None else o + part
        o = o * lax.rsqrt(jnp.mean(o * o, axis=-1, keepdims=True) + _EPS) * g * (1.0 - lam_init)
        o_ref[:, c0:c0 + hw] = o.astype(o_ref.dtype)


def _attention(qkv, lam_vecs, subln_g, n_heads, qk, seq, lam_init, cache=None, cast_src=None):
    r = qkv.shape[0]
    d_att = n_heads * 2 * qk
    tq = _tile(seq, 256)
    q_tiles = seq // tq
    in_specs = [pl.BlockSpec((tq, d_att), lambda b, t: (b * q_tiles + t, 0)),
                pl.BlockSpec((seq, d_att), lambda b, t: (b, 1)),
                pl.BlockSpec((seq, d_att), lambda b, t: (b, 2))]
    args = [qkv, qkv, qkv]
    if cache is not None:
        ck, cv = cache
        in_specs += [pl.BlockSpec((None,) + ck.shape[1:], lambda b, t: (b, 0, 0)),
                     pl.BlockSpec((None,) + cv.shape[1:], lambda b, t: (b, 0, 0, 0))]
        args += [ck, cv]
        scratch = [pltpu.VMEM((cv.shape[1], d_att), _BF16)] * 2
    else:
        scratch = []
    in_specs += [pl.BlockSpec(lam_vecs.shape, lambda b, t: (0, 0)),
                 pl.BlockSpec((1, 2 * qk), lambda b, t: (0, 0))]
    args += [lam_vecs, subln_g.reshape(1, 2 * qk)]
    out_specs = pl.BlockSpec((tq, d_att), lambda b, t: (b * q_tiles + t, 0))
    out_shape = jax.ShapeDtypeStruct((r, d_att), _BF16)
    side_cast = None
    if cast_src is not None:
        side_cast = _SideCast(cast_src, r // seq, q_tiles)
        in_specs.append(side_cast.spec)
        args.append(cast_src)
        out_specs, out_shape = [out_specs, side_cast.out_spec], [out_shape, side_cast.out_shape]
    return pl.pallas_call(
        functools.partial(_attn_kernel, n_heads=n_heads, qk=qk, lam_init=lam_init,
                          cached=cache is not None, side_cast=side_cast),
        grid=(r // seq, q_tiles),
        in_specs=in_specs, out_specs=out_specs, out_shape=out_shape,
        scratch_shapes=scratch,
        compiler_params=_params("parallel", "arbitrary"),
        name="diff_attention_latent" if cache is not None else "diff_attention_context",
    )(*args)


def _outproj_kernel(a_ref, c_ref, wa_ref, wc_ref, x_ref, mod_ref, *rest, gate_idx, side_cast):
    if side_cast is None:
        o_ref, = rest
    else:
        src_ref, o_ref, dst_ref = rest
        side_cast.run(src_ref, dst_ref)
    y = (jnp.dot(a_ref[...], wa_ref[...], preferred_element_type=_F32)
         + jnp.dot(c_ref[...], wc_ref[...], preferred_element_type=_F32))
    o_ref[...] = x_ref[...] + mod_ref[gate_idx:gate_idx + 1, :] * y


def _out_proj(attn, conv, w_out, x, mod3, gate_idx, cast_src=None):
    r, d = x.shape
    d_att, d_conv = attn.shape[1], conv.shape[1]
    assert d_att == d_conv
    rows_per_group = r // mod3.shape[0]
    tm = _tile(rows_per_group, 1024)
    tn = _tile(d, 512 if cast_src is not None else 1024)
    in_specs = [pl.BlockSpec((tm, d_att), lambda i, j: (i, 0)),
                pl.BlockSpec((tm, d_conv), lambda i, j: (i, 0)),
                pl.BlockSpec((d_att, tn), lambda i, j: (0, j)),
                pl.BlockSpec((d_conv, tn), lambda i, j: (1, j)),
                pl.BlockSpec((tm, tn), lambda i, j: (i, j)),
                pl.BlockSpec((None, _N_MOD, tn), lambda i, j: ((i * tm) // rows_per_group, 0, j))]
    args = [attn, conv, w_out, w_out, x, mod3]
    out_specs = pl.BlockSpec((tm, tn), lambda i, j: (i, j))
    out_shape = jax.ShapeDtypeStruct((r, d), _F32)
    side_cast = None
    if cast_src is not None:
        side_cast = _SideCast(cast_src, r // tm, d // tn, row_part=1, n_row_parts=2)
        in_specs.append(side_cast.spec)
        args.append(cast_src)
        out_specs, out_shape = [out_specs, side_cast.out_spec], [out_shape, side_cast.out_shape]
    return pl.pallas_call(
        functools.partial(_outproj_kernel, gate_idx=gate_idx, side_cast=side_cast),
        grid=(r // tm, d // tn),
        in_specs=in_specs, out_specs=out_specs, out_shape=out_shape,
        compiler_params=_params("parallel", "arbitrary"),
        name="out_proj",
    )(*args)


def _mlp_kernel(h_ref, w1_ref, w2a_ref, w2b_ref, x_ref, mod_ref, o_ref, hid_ref, *, gate_idx, n_f, tf):
    s = pl.program_id(1)

    @pl.when(s < n_f)
    def _():
        hid = jnp.dot(h_ref[...], w1_ref[...], preferred_element_type=_F32)
        col = pl.multiple_of(s * tf, tf)
        hid_ref[:, pl.ds(col, tf)] = jnp.square(jnp.maximum(hid, 0.0)).astype(hid_ref.dtype)

    @pl.when(s >= n_f)
    def _():
        half = w2a_ref.shape[0]
        y = (jnp.dot(hid_ref[:, :half], w2a_ref[...], preferred_element_type=_F32)
             + jnp.dot(hid_ref[:, half:], w2b_ref[...], preferred_element_type=_F32))
        o_ref[...] = x_ref[...] + mod_ref[gate_idx:gate_idx + 1, :] * y


def _mlp(h, w1, w2_halves, x, mod3, gate_idx):
    r, d = x.shape
    d_ff = w1.shape[1]
    rows_per_group = r // mod3.shape[0]
    tm = _tile(rows_per_group, 512)
    tf = _tile(d_ff, 1024)
    tn = _tile(d, 256)
    n_f = d_ff // tf

    def out_col(s):
        return jnp.maximum(s - n_f, 0)

    return pl.pallas_call(
        functools.partial(_mlp_kernel, gate_idx=gate_idx, n_f=n_f, tf=tf),
        grid=(r // tm, n_f + d // tn),
        in_specs=[pl.BlockSpec((tm, d), lambda i, s: (i, 0)),
                  pl.BlockSpec((d, tf), lambda i, s: (0, jnp.minimum(s, n_f - 1))),
                  pl.BlockSpec((d_ff // 2, tn), lambda i, s: (0, out_col(s))),
                  pl.BlockSpec((d_ff // 2, tn), lambda i, s: (0, out_col(s))),
                  pl.BlockSpec((tm, tn), lambda i, s: (i, out_col(s))),
                  pl.BlockSpec((None, _N_MOD, tn),
                               lambda i, s: ((i * tm) // rows_per_group, 0, out_col(s)))],
        out_specs=pl.BlockSpec((tm, tn), lambda i, s: (i, out_col(s))),
        out_shape=jax.ShapeDtypeStruct((r, d), _F32),
        scratch_shapes=[pltpu.VMEM((tm, d_ff), _BF16)],
        compiler_params=_params("parallel", "arbitrary"),
        name="mlp",
    )(h, w1, *w2_halves, x, mod3)


def _rope_tables(n_tokens, qk):
    axis_dim = qk // 2
    rows = n_tokens // _GRID_W
    row = jnp.repeat(jnp.arange(rows, dtype=_F32), _GRID_W)
    col = jnp.tile(jnp.arange(_GRID_W, dtype=_F32), rows)
    inv = jnp.power(_ROPE_THETA, -jnp.arange(0, axis_dim, 2, dtype=_F32) / axis_dim)
    row_ang, col_ang = row[:, None] * inv, col[:, None] * inv
    cos = jnp.concatenate([jnp.cos(row_ang)] * 2 + [jnp.cos(col_ang)] * 2, axis=-1)
    sin = jnp.concatenate([-jnp.sin(row_ang), jnp.sin(row_ang),
                           -jnp.sin(col_ang), jnp.sin(col_ang)], axis=-1)
    return cos, sin


def _trunk(x, mod3, seq, w, n_heads, qk, lam_init, rope=None, cache=None):
    d_att = n_heads * 2 * qk
    context = rope is None
    h = _prenorm(x, w["norm_attn_g"], mod3, 0, 1)
    if context:
        w = dict(w)
        assert w["w_in"].shape[1] == 2 * w["w_qkv"].shape[1]
        qkv, new_k, new_v, w["w_conv"], w2_top = _qkv_proj(h, w["w_qkv"], w["q_norm_g"], w["k_norm_g"], d_att, qk,
                                                   side_cast_args=[(w["w_in"], 1, 2),
                                                                   (w["w_mlp_out"], 0, 1, 0, 2)])
        attn, w["w_out"] = _attention(qkv, w["lam_vecs"], w["subln_g"], n_heads, qk, seq, lam_init,
                                      cast_src=w["w_out"])
        h, attn = lax.optimization_barrier((h, attn))
        conv, w["w_mlp_in"] = _conv_proj(h, w["w_conv"], w["conv_w"], seq, cast_src=w["w_mlp_in"])
    else:
        qkv = _qkv_proj(h, w["w_qkv"], w["q_norm_g"], w["k_norm_g"], d_att, qk, rope)
        conv = _conv_proj(h, w["w_conv"], w["conv_w"], seq)
        attn = _attention(qkv, w["lam_vecs"], w["subln_g"], n_heads, qk, seq, lam_init, cache)
        new_k = new_v = None
    if context:
        x2, w2_bottom = _out_proj(attn, conv, w["w_out"], x, mod3, 2, cast_src=w["w_mlp_out"])
        w["w_mlp_out"] = (w2_top, w2_bottom)
    else:
        x2 = _out_proj(attn, conv, w["w_out"], x, mod3, 2)
    h2 = _prenorm(x2, w["norm_mlp_g"], mod3, 3, 4)
    y = _mlp(h2, w["w_mlp_in"], w["w_mlp_out"], x2, mod3, 5)
    return y, new_k, new_v, w


def kernel(x_prompt, x_sample, cache_k, cache_v, c, c_ctx, w_ada, b_ada, norm_attn_g, w_in, q_norm_g, k_norm_g, lambda_q1, lambda_k1, lambda_q2, lambda_k2, subln_g, conv_w, w_out, norm_mlp_g, w_mlp_in, w_mlp_out):
    assert w_in.shape[0] == 1, "one trunk layer"
    batch, seq, d = x_prompt.shape
    dec_batch, dec_seq, _ = x_sample.shape
    past, n_heads, _, qk = cache_k.shape[2:]
    d_att = n_heads * 2 * qk
    layer = 0
    lam_init = 0.8 - 0.6 * math.exp(-0.3 * layer)

    w = {
        "norm_attn_g": norm_attn_g[layer], "q_norm_g": q_norm_g[layer], "k_norm_g": k_norm_g[layer],
        "subln_g": subln_g[layer], "conv_w": conv_w[layer], "norm_mlp_g": norm_mlp_g[layer],
        "lam_vecs": jnp.stack([lambda_q1[layer], lambda_k1[layer], lambda_q2[layer], lambda_k2[layer]]),
        "w_in": w_in[layer], "w_qkv": w_in[layer][:, :3 * d_att].astype(_BF16), "w_out": w_out[layer],
        "w_mlp_in": w_mlp_in[layer], "w_mlp_out": w_mlp_out[layer],
    }

    n_cond = 1 + dec_batch
    cond = jnp.concatenate([c_ctx[None], c, jnp.zeros((-n_cond % 8, d), _F32)], axis=0)
    mod = _modulation(cond, w_ada[layer], b_ada[layer]).reshape(cond.shape[0], _N_MOD, d)

    y_p, new_k, new_v, w = _trunk(x_prompt.reshape(batch * seq, d), mod[0:1], seq, w,
                                  n_heads, qk, lam_init)
    cache = (cache_k[:, layer].reshape(dec_batch, past * n_heads * 2, qk), cache_v[:, layer])
    x_sample, w["w_mlp_in"] = lax.optimization_barrier((x_sample, w["w_mlp_in"]))
    y_s = _trunk(x_sample.reshape(dec_batch * dec_seq, d), mod[1:n_cond], dec_seq, w,
                 n_heads, qk, lam_init, rope=_rope_tables(dec_seq, qk), cache=cache)[0]

    return (y_p.reshape(batch, seq, d),
            y_s.reshape(dec_batch, dec_seq, d),
            new_k.reshape(batch, 1, seq, n_heads, 2, qk),
            new_v.reshape(batch, 1, seq, n_heads, 2 * qk))
```

```python
import functools
import math

import jax
import jax.numpy as jnp
from jax import lax
from jax.experimental import pallas as pl
from jax.experimental.pallas import tpu as pltpu

_GRID_W = 64
_ROPE_THETA = 10000.0
_EPS = 1e-6
_N_MOD = 6
_LANES = 128
_VMEM_LIMIT = 63 * 1024 * 1024

_F32 = jnp.float32
_BF16 = jnp.bfloat16


def _tile(dim, pref):
    t = min(dim, pref)
    assert dim % t == 0, (dim, pref)
    return t


def _params(*sem):
    return pltpu.CompilerParams(dimension_semantics=sem, vmem_limit_bytes=_VMEM_LIMIT)


class _SideCast:
    def __init__(self, src, n_i, n_j, col_block=0, n_col_blocks=1, row_part=0, n_row_parts=1):
        rows, cols = src.shape[0] // n_row_parts, src.shape[1] // n_col_blocks
        n_blocks = 1
        while n_blocks * 2 <= n_i * n_j and rows % (n_blocks * 2) == 0 and rows // (n_blocks * 2) >= 16:
            n_blocks *= 2
        self.src, self.n_j, self.n_blocks, self.all_steps = src, n_j, n_blocks, n_blocks == n_i * n_j
        block = (rows // n_blocks, cols)

        def row_block(*idx):
            step = idx[0] * n_j + idx[1] if len(idx) > 1 else idx[0]
            return jnp.minimum(step, n_blocks - 1)

        self.spec = pl.BlockSpec(block, lambda *idx: (row_part * n_blocks + row_block(*idx), col_block))
        self.out_spec = pl.BlockSpec(block, lambda *idx: (row_block(*idx), 0))
        self.out_shape = jax.ShapeDtypeStruct((rows, cols), _BF16)

    def run(self, src_ref, dst_ref):
        def cast():
            dst_ref[...] = src_ref[...].astype(dst_ref.dtype)
        if self.all_steps:
            cast()
        else:
            pl.when(pl.program_id(0) * self.n_j + pl.program_id(1) < self.n_blocks)(cast)


def _mod_kernel(cond_ref, w_ref, b_ref, o_ref):
    c = cond_ref[...]
    s = c / (1.0 + jnp.exp(-c))
    o_ref[...] = jnp.dot(s, w_ref[...], preferred_element_type=_F32) + b_ref[...]


def _modulation(cond, w_ada, b_ada):
    r, d = cond.shape
    n = w_ada.shape[1]
    tn = _tile(n, 512)
    return pl.pallas_call(
        _mod_kernel,
        grid=(n // tn,),
        in_specs=[pl.BlockSpec((r, d), lambda j: (0, 0)),
                  pl.BlockSpec((d, tn), lambda j: (0, j)),
                  pl.BlockSpec((1, tn), lambda j: (0, j))],
        out_specs=pl.BlockSpec((r, tn), lambda j: (0, j)),
        out_shape=jax.ShapeDtypeStruct((r, n), _F32),
        compiler_params=_params("arbitrary"),
        name="modulation",
    )(cond, w_ada, b_ada.reshape(1, n))


def _prenorm_kernel(x_ref, g_ref, mod_ref, o_ref, *, shift_idx, scale_idx):
    x = x_ref[...]
    gain = g_ref[...] * (1.0 + mod_ref[scale_idx:scale_idx + 1, :])
    shift = mod_ref[shift_idx:shift_idx + 1, :]
    r = lax.rsqrt(jnp.mean(x * x, axis=-1, keepdims=True) + _EPS)
    o_ref[...] = (x * r * gain + shift).astype(o_ref.dtype)


def _prenorm(x, g, mod3, shift_idx, scale_idx):
    r, d = x.shape
    rows_per_group = r // mod3.shape[0]
    tr = _tile(rows_per_group, 512)
    return pl.pallas_call(
        functools.partial(_prenorm_kernel, shift_idx=shift_idx, scale_idx=scale_idx),
        grid=(r // tr,),
        in_specs=[pl.BlockSpec((tr, d), lambda i: (i, 0)),
                  pl.BlockSpec((1, d), lambda i: (0, 0)),
                  pl.BlockSpec((None, _N_MOD, d), lambda i: ((i * tr) // rows_per_group, 0, 0))],
        out_specs=pl.BlockSpec((tr, d), lambda i: (i, 0)),
        out_shape=jax.ShapeDtypeStruct((r, d), _BF16),
        compiler_params=_params("parallel"),
        name="prenorm",
    )(x, g.reshape(1, d), mod3)


def _headnorm(xc, g, scale):
    r = lax.rsqrt(jnp.mean(xc * xc, axis=-1, keepdims=True) + _EPS)
    if scale != 1.0:
        r = r * scale
    return xc * r * g


def _rope(y, cos, sin_signed, first_half):
    qk = y.shape[-1]
    ahead = pltpu.roll(y, qk - qk // 4, axis=1)
    behind = pltpu.roll(y, qk // 4, axis=1)
    return y * cos + jnp.where(first_half, ahead, behind) * sin_signed


def _qkv_kernel(*refs, latent, qk, n_sec, q_scale, side_cast):
    if latent:
        x_ref, w_ref, qg_ref, kg_ref, cos_ref, sin_ref, qkv_ref = refs
    else:
        n_cast = len(side_cast)
        x_ref, w_ref, qg_ref, kg_ref = refs[:4]
        qkv_ref, newk_ref, newv_ref = refs[4 + n_cast:7 + n_cast]
        for cast, src_ref, dst_ref in zip(side_cast, refs[4:4 + n_cast], refs[7 + n_cast:]):
            cast.run(src_ref, dst_ref)
    j = pl.program_id(1)
    tn = w_ref.shape[1]

    def project():
        return jnp.dot(x_ref[...], w_ref[...], preferred_element_type=_F32)

    if latent:
        shared = project()
        project = lambda: shared

    def qk_epilogue(g_ref, scale, f32_ref):
        acc = project()
        g = g_ref[...]
        if latent:
            cos = cos_ref[...]
            sin = sin_ref[...]
            lane = lax.broadcasted_iota(jnp.int32, (1, qk), 1)
            first_half = (lane % (qk // 2)) < (qk // 4)
        for c in range(tn // qk):
            cols = slice(c * qk, (c + 1) * qk)
            y = _headnorm(acc[:, cols], g, scale)
            if latent:
                y = _rope(y, cos, sin, first_half)
            if f32_ref is not None:
                n_chunks = n_sec * (tn // qk)
                chunk = (j - n_sec) * (tn // qk) + c
                f32_ref[pl.ds(chunk, acc.shape[0], stride=n_chunks), :] = y
            qkv_ref[:, cols] = y.astype(qkv_ref.dtype)

    @pl.when(j < n_sec)
    def _():
        qk_epilogue(qg_ref, q_scale, None)

    @pl.when((j >= n_sec) & (j < 2 * n_sec))
    def _():
        qk_epilogue(kg_ref, 1.0, None if latent else newk_ref)

    @pl.when(j >= 2 * n_sec)
    def _():
        acc = project()
        qkv_ref[...] = acc.astype(qkv_ref.dtype)
        if not latent:
            newv_ref[...] = acc


def _qkv_proj(h, w_qkv, q_g, k_g, d_att, qk, rope=None, side_cast_args=None):
    r, d = h.shape
    latent = rope is not None
    tm = _tile(r if not latent else rope[0].shape[0], 1024)
    tn = _tile(d_att, 1024 if latent else 512)
    n_sec = d_att // tn
    in_specs = [pl.BlockSpec((tm, d), lambda i, j: (i, 0)),
                pl.BlockSpec((d, tn), lambda i, j: (0, j)),
                pl.BlockSpec((1, qk), lambda i, j: (0, 0)),
                pl.BlockSpec((1, qk), lambda i, j: (0, 0))]
    args = [h, w_qkv, q_g.reshape(1, qk), k_g.reshape(1, qk)]
    qkv_spec = pl.BlockSpec((tm, tn), lambda i, j: (i, j))
    qkv_shape = jax.ShapeDtypeStruct((r, 3 * d_att), _BF16)
    if latent:
        seq_tiles = rope[0].shape[0] // tm
        rope_spec = pl.BlockSpec((tm, qk), lambda i, j: (i % seq_tiles, 0))
        in_specs += [rope_spec, rope_spec]
        args += list(rope)
        out_specs, out_shape = qkv_spec, qkv_shape
        side_cast = None
    else:
        side_cast = [_SideCast(a[0], r // tm, 3 * n_sec, *a[1:]) for a in side_cast_args]
        in_specs += [c.spec for c in side_cast]
        args += [c.src for c in side_cast]
        n_chunks = d_att // qk
        out_specs = [qkv_spec,
                     pl.BlockSpec((tm * n_chunks, qk), lambda i, j: (i, 0)),
                     pl.BlockSpec((tm, tn), lambda i, j: (i, jnp.clip(j - 2 * n_sec, 0, n_sec - 1)))]
        out_specs += [c.out_spec for c in side_cast]
        out_shape = [qkv_shape,
                     jax.ShapeDtypeStruct((r * n_chunks, qk), _F32),
                     jax.ShapeDtypeStruct((r, d_att), _F32)]
        out_shape += [c.out_shape for c in side_cast]
    return pl.pallas_call(
        functools.partial(_qkv_kernel, latent=latent, qk=qk, n_sec=n_sec, q_scale=qk ** -0.5,
                          side_cast=side_cast),
        grid=(r // tm, 3 * n_sec),
        in_specs=in_specs, out_specs=out_specs, out_shape=out_shape,
        compiler_params=_params("parallel", "arbitrary"),
        name="qkv_proj_latent" if latent else "qkv_proj_context",
    )(*args)


def _conv_kernel(x_ref, wb_ref, wc_ref, wu_ref, cw_ref, *rest, seq, side_cast):
    if side_cast is None:
        o_ref, = rest
    else:
        src_ref, o_ref, dst_ref = rest
        side_cast.run(src_ref, dst_ref)
    x = x_ref[...]
    b = jnp.dot(x, wb_ref[...], preferred_element_type=_F32)
    c = jnp.dot(x, wc_ref[...], preferred_element_type=_F32)
    u = jnp.dot(x, wu_ref[...], preferred_element_type=_F32)
    z = c * u
    tm = z.shape[0]
    pos = lax.broadcasted_iota(jnp.int32, (tm, 1), 0) % seq
    z_prev = jnp.where(pos == 0, 0.0, pltpu.roll(z, 1, axis=0))
    z_next = jnp.where(pos == seq - 1, 0.0, pltpu.roll(z, tm - 1, axis=0))
    cw = cw_ref[...]
    y = cw[0:1, :] * z_prev + cw[1:2, :] * z + cw[2:3, :] * z_next
    o_ref[...] = (b * y).astype(o_ref.dtype)


def _conv_proj(h, w_conv, conv_w, seq, cast_src=None):
    r, d = h.shape
    d_conv = conv_w.shape[1]
    tm = _tile(r, max(seq, 1024))
    assert tm % seq == 0
    tc = _tile(d_conv, 256 if cast_src is not None else 512)
    nc = d_conv // tc

    def w_spec(sec):
        return pl.BlockSpec((d, tc), lambda i, j: (0, sec * nc + j))

    in_specs = [pl.BlockSpec((tm, d), lambda i, j: (i, 0)),
                w_spec(0), w_spec(1), w_spec(2),
                pl.BlockSpec((conv_w.shape[0], tc), lambda i, j: (0, j))]
    args = [h, w_conv, w_conv, w_conv, conv_w]
    out_specs = pl.BlockSpec((tm, tc), lambda i, j: (i, j))
    out_shape = jax.ShapeDtypeStruct((r, d_conv), _BF16)
    side_cast = None
    if cast_src is not None:
        side_cast = _SideCast(cast_src, r // tm, nc)
        in_specs.append(side_cast.spec)
        args.append(cast_src)
        out_specs, out_shape = [out_specs, side_cast.out_spec], [out_shape, side_cast.out_shape]
    return pl.pallas_call(
        functools.partial(_conv_kernel, seq=seq, side_cast=side_cast),
        grid=(r // tm, nc),
        in_specs=in_specs, out_specs=out_specs, out_shape=out_shape,
        compiler_params=_params("parallel", "arbitrary"),
        name="conv_proj",
    )(*args)


def _attn_kernel(*refs, n_heads, qk, lam_init, cached, side_cast):
    hw = 2 * qk
    if cached:
        q_ref, k_ref, v_ref, ck_ref, cv_ref, lam_ref, g_ref, o_ref, ckb_ref, cvb_ref = refs
        past = cv_ref.shape[0]

        @pl.when(pl.program_id(1) == 0)
        def _():
            for h in range(n_heads):
                for c in range(2):
                    rows = pl.ds(2 * h + c, past, stride=2 * n_heads)
                    ckb_ref[:, h * hw + c * qk:h * hw + (c + 1) * qk] = ck_ref[rows, :].astype(_BF16)
                cvb_ref[:, h * hw:(h + 1) * hw] = cv_ref[:, h, :].astype(_BF16)
    elif side_cast is not None:
        q_ref, k_ref, v_ref, lam_ref, g_ref, src_ref, o_ref, dst_ref = refs
        side_cast.run(src_ref, dst_ref)
    else:
        q_ref, k_ref, v_ref, lam_ref, g_ref, o_ref = refs
    lv = lam_ref[...]
    lam = (jnp.exp(jnp.sum(lv[0:1] * lv[1:2], axis=-1, keepdims=True))
           - jnp.exp(jnp.sum(lv[2:3] * lv[3:4], axis=-1, keepdims=True)) + lam_init)
    g = g_ref[...]
    dims = (((1,), (1,)), ((), ()))

    def probs(q, keys):
        s = [lax.dot_general(q, k, dims, preferred_element_type=_F32) for k in keys]
        m = functools.reduce(jnp.maximum, [jnp.max(x, axis=-1, keepdims=True) for x in s])
        e = [jnp.exp(x - m) for x in s]
        den = functools.reduce(jnp.add, [jnp.sum(x, axis=-1, keepdims=True) for x in e])
        return e, 1.0 / den

    for h in range(n_heads):
        o = None
        c0, c1 = h * hw, h * hw + qk
        keys0 = [k_ref[:, c0:c0 + qk]]
        keys1 = [k_ref[:, c1:c1 + qk]]
        vals = [v_ref[:, c0:c0 + hw]]
        if cached:
            keys0.append(ckb_ref[:, c0:c0 + qk])
            keys1.append(ckb_ref[:, c1:c1 + qk])
            vals.append(cvb_ref[:, c0:c0 + hw])
        e0, r0 = probs(q_ref[:, c0:c0 + qk], keys0)
        e1, r1 = probs(q_ref[:, c1:c1 + qk], keys1)
        r1 = r1 * lam
        for p0, p1, v in zip(e0, e1, vals):
            a = (p0 * r0 - p1 * r1).astype(_BF16)
            part = jnp.dot(a, v, preferred_element_type=_F32)
            o = part if o is None else o + part
        o = o * lax.rsqrt(jnp.mean(o * o, axis=-1, keepdims=True) + _EPS) * g * (1.0 - lam_init)
        o_ref[:, c0:c0 + hw] = o.astype(o_ref.dtype)


def _attention(qkv, lam_vecs, subln_g, n_heads, qk, seq, lam_init, cache=None, cast_src=None):
    r = qkv.shape[0]
    d_att = n_heads * 2 * qk
    tq = _tile(seq, 256)
    q_tiles = seq // tq
    in_specs = [pl.BlockSpec((tq, d_att), lambda b, t: (b * q_tiles + t, 0)),
                pl.BlockSpec((seq, d_att), lambda b, t: (b, 1)),
                pl.BlockSpec((seq, d_att), lambda b, t: (b, 2))]
    args = [qkv, qkv, qkv]
    if cache is not None:
        ck, cv = cache
        in_specs += [pl.BlockSpec((None,) + ck.shape[1:], lambda b, t: (b, 0, 0)),
                     pl.BlockSpec((None,) + cv.shape[1:], lambda b, t: (b, 0, 0, 0))]
        args += [ck, cv]
        scratch = [pltpu.VMEM((cv.shape[1], d_att), _BF16)] * 2
    else:
        scratch = []
    in_specs += [pl.BlockSpec(lam_vecs.shape, lambda b, t: (0, 0)),
                 pl.BlockSpec((1, 2 * qk), lambda b, t: (0, 0))]
    args += [lam_vecs, subln_g.reshape(1, 2 * qk)]
    out_specs = pl.BlockSpec((tq, d_att), lambda b, t: (b * q_tiles + t, 0))
    out_shape = jax.ShapeDtypeStruct((r, d_att), _BF16)
    side_cast = None
    if cast_src is not None:
        side_cast = _SideCast(cast_src, r // seq, q_tiles)
        in_specs.append(side_cast.spec)
        args.append(cast_src)
        out_specs, out_shape = [out_specs, side_cast.out_spec], [out_shape, side_cast.out_shape]
    return pl.pallas_call(
        functools.partial(_attn_kernel, n_heads=n_heads, qk=qk, lam_init=lam_init,
                          cached=cache is not None, side_cast=side_cast),
        grid=(r // seq, q_tiles),
        in_specs=in_specs, out_specs=out_specs, out_shape=out_shape,
        scratch_shapes=scratch,
        compiler_params=_params("parallel", "arbitrary"),
        name="diff_attention_latent" if cache is not None else "diff_attention_context",
    )(*args)


def _outproj_kernel(a_ref, c_ref, wa_ref, wc_ref, x_ref, mod_ref, *rest, gate_idx, side_cast):
    if side_cast is None:
        o_ref, = rest
    else:
        src_ref, o_ref, dst_ref = rest
        side_cast.run(src_ref, dst_ref)
    y = (jnp.dot(a_ref[...], wa_ref[...], preferred_element_type=_F32)
         + jnp.dot(c_ref[...], wc_ref[...], preferred_element_type=_F32))
    o_ref[...] = x_ref[...] + mod_ref[gate_idx:gate_idx + 1, :] * y


def _out_proj(attn, conv, w_out, x, mod3, gate_idx, cast_src=None):
    r, d = x.shape
    d_att, d_conv = attn.shape[1], conv.shape[1]
    assert d_att == d_conv
    rows_per_group = r // mod3.shape[0]
    tm = _tile(rows_per_group, 1024)
    tn = _tile(d, 512 if cast_src is not None else 1024)
    in_specs = [pl.BlockSpec((tm, d_att), lambda i, j: (i, 0)),
                pl.BlockSpec((tm, d_conv), lambda i, j: (i, 0)),
                pl.BlockSpec((d_att, tn), lambda i, j: (0, j)),
                pl.BlockSpec((d_conv, tn), lambda i, j: (1, j)),
                pl.BlockSpec((tm, tn), lambda i, j: (i, j)),
                pl.BlockSpec((None, _N_MOD, tn), lambda i, j: ((i * tm) // rows_per_group, 0, j))]
    args = [attn, conv, w_out, w_out, x, mod3]
    out_specs = pl.BlockSpec((tm, tn), lambda i, j: (i, j))
    out_shape = jax.ShapeDtypeStruct((r, d), _F32)
    side_cast = None
    if cast_src is not None:
        side_cast = _SideCast(cast_src, r // tm, d // tn, row_part=1, n_row_parts=2)
        in_specs.append(side_cast.spec)
        args.append(cast_src)
        out_specs, out_shape = [out_specs, side_cast.out_spec], [out_shape, side_cast.out_shape]
    return pl.pallas_call(
        functools.partial(_outproj_kernel, gate_idx=gate_idx, side_cast=side_cast),
        grid=(r // tm, d // tn),
        in_specs=in_specs, out_specs=out_specs, out_shape=out_shape,
        compiler_params=_params("parallel", "arbitrary"),
        name="out_proj",
    )(*args)


def _mlp_kernel(h_ref, w1_ref, w2a_ref, w2b_ref, x_ref, mod_ref, o_ref, hid_ref, *, gate_idx, n_f, tf):
    s = pl.program_id(1)

    @pl.when(s < n_f)
    def _():
        hid = jnp.dot(h_ref[...], w1_ref[...], preferred_element_type=_F32)
        col = pl.multiple_of(s * tf, tf)
        hid_ref[:, pl.ds(col, tf)] = jnp.square(jnp.maximum(hid, 0.0)).astype(hid_ref.dtype)

    @pl.when(s >= n_f)
    def _():
        half = w2a_ref.shape[0]
        y = (jnp.dot(hid_ref[:, :half], w2a_ref[...], preferred_element_type=_F32)
             + jnp.dot(hid_ref[:, half:], w2b_ref[...], preferred_element_type=_F32))
        o_ref[...] = x_ref[...] + mod_ref[gate_idx:gate_idx + 1, :] * y


def _mlp(h, w1, w2_halves, x, mod3, gate_idx):
    r, d = x.shape
    d_ff = w1.shape[1]
    rows_per_group = r // mod3.shape[0]
    tm = _tile(rows_per_group, 512)
    tf = _tile(d_ff, 1024)
    tn = _tile(d, 256)
    n_f = d_ff // tf

    def out_col(s):
        return jnp.maximum(s - n_f, 0)

    return pl.pallas_call(
        functools.partial(_mlp_kernel, gate_idx=gate_idx, n_f=n_f, tf=tf),
        grid=(r // tm, n_f + d // tn),
        in_specs=[pl.BlockSpec((tm, d), lambda i, s: (i, 0)),
                  pl.BlockSpec((d, tf), lambda i, s: (0, jnp.minimum(s, n_f - 1))),
                  pl.BlockSpec((d_ff // 2, tn), lambda i, s: (0, out_col(s))),
                  pl.BlockSpec((d_ff // 2, tn), lambda i, s: (0, out_col(s))),
                  pl.BlockSpec((tm, tn), lambda i, s: (i, out_col(s))),
                  pl.BlockSpec((None, _N_MOD, tn),
                               lambda i, s: ((i * tm) // rows_per_group, 0, out_col(s)))],
        out_specs=pl.BlockSpec((tm, tn), lambda i, s: (i, out_col(s))),
        out_shape=jax.ShapeDtypeStruct((r, d), _F32),
        scratch_shapes=[pltpu.VMEM((tm, d_ff), _BF16)],
        compiler_params=_params("parallel", "arbitrary"),
        name="mlp",
    )(h, w1, *w2_halves, x, mod3)


def _rope_tables(n_tokens, qk):
    axis_dim = qk // 2
    rows = n_tokens // _GRID_W
    row = jnp.repeat(jnp.arange(rows, dtype=_F32), _GRID_W)
    col = jnp.tile(jnp.arange(_GRID_W, dtype=_F32), rows)
    inv = jnp.power(_ROPE_THETA, -jnp.arange(0, axis_dim, 2, dtype=_F32) / axis_dim)
    row_ang, col_ang = row[:, None] * inv, col[:, None] * inv
    cos = jnp.concatenate([jnp.cos(row_ang)] * 2 + [jnp.cos(col_ang)] * 2, axis=-1)
    sin = jnp.concatenate([-jnp.sin(row_ang), jnp.sin(row_ang),
                           -jnp.sin(col_ang), jnp.sin(col_ang)], axis=-1)
    return cos, sin


def _trunk(x, mod3, seq, w, n_heads, qk, lam_init, rope=None, cache=None):
    d_att = n_heads * 2 * qk
    context = rope is None
    h = _prenorm(x, w["norm_attn_g"], mod3, 0, 1)
    if context:
        w = dict(w)
        assert w["w_in"].shape[1] == 2 * w["w_qkv"].shape[1]
        qkv, new_k, new_v, w["w_conv"], w2_top = _qkv_proj(h, w["w_qkv"], w["q_norm_g"], w["k_norm_g"], d_att, qk,
                                                   side_cast_args=[(w["w_in"], 1, 2),
                                                                   (w["w_mlp_out"], 0, 1, 0, 2)])
        attn, w["w_out"] = _attention(qkv, w["lam_vecs"], w["subln_g"], n_heads, qk, seq, lam_init,
                                      cast_src=w["w_out"])
        h, attn = lax.optimization_barrier((h, attn))
        conv, w["w_mlp_in"] = _conv_proj(h, w["w_conv"], w["conv_w"], seq, cast_src=w["w_mlp_in"])
    else:
        qkv = _qkv_proj(h, w["w_qkv"], w["q_norm_g"], w["k_norm_g"], d_att, qk, rope)
        conv = _conv_proj(h, w["w_conv"], w["conv_w"], seq)
        attn = _attention(qkv, w["lam_vecs"], w["subln_g"], n_heads, qk, seq, lam_init, cache)
        new_k = new_v = None
    if context:
        x2, w2_bottom = _out_proj(attn, conv, w["w_out"], x, mod3, 2, cast_src=w["w_mlp_out"])
        w["w_mlp_out"] = (w2_top, w2_bottom)
    else:
        x2 = _out_proj(attn, conv, w["w_out"], x, mod3, 2)
    h2 = _prenorm(x2, w["norm_mlp_g"], mod3, 3, 4)
    y = _mlp(h2, w["w_mlp_in"], w["w_mlp_out"], x2, mod3, 5)
    return y, new_k, new_v, w


def kernel(x_prompt, x_sample, cache_k, cache_v, c, c_ctx, w_ada, b_ada, norm_attn_g, w_in, q_norm_g, k_norm_g, lambda_q1, lambda_k1, lambda_q2, lambda_k2, subln_g, conv_w, w_out, norm_mlp_g, w_mlp_in, w_mlp_out):
    assert w_in.shape[0] == 1, "one trunk layer"
    batch, seq, d = x_prompt.shape
    dec_batch, dec_seq, _ = x_sample.shape
    past, n_heads, _, qk = cache_k.shape[2:]
    d_att = n_heads * 2 * qk
    layer = 0
    lam_init = 0.8 - 0.6 * math.exp(-0.3 * layer)

    w = {
        "norm_attn_g": norm_attn_g[layer], "q_norm_g": q_norm_g[layer], "k_norm_g": k_norm_g[layer],
        "subln_g": subln_g[layer], "conv_w": conv_w[layer], "norm_mlp_g": norm_mlp_g[layer],
        "lam_vecs": jnp.stack([lambda_q1[layer], lambda_k1[layer], lambda_q2[layer], lambda_k2[layer]]),
        "w_in": w_in[layer], "w_qkv": w_in[layer][:, :3 * d_att].astype(_BF16), "w_out": w_out[layer],
        "w_mlp_in": w_mlp_in[layer], "w_mlp_out": w_mlp_out[layer],
    }

    n_cond = 1 + dec_batch
    cond = jnp.concatenate([c_ctx[None], c, jnp.zeros((-n_cond % 8, d), _F32)], axis=0)
    mod = _modulation(cond, w_ada[layer], b_ada[layer]).reshape(cond.shape[0], _N_MOD, d)

    y_p, new_k, new_v, w = _trunk(x_prompt.reshape(batch * seq, d), mod[0:1], seq, w,
                                  n_heads, qk, lam_init)
    cache = (cache_k[:, layer].reshape(dec_batch, past * n_heads * 2, qk), cache_v[:, layer])
    x_sample, w["w_mlp_in"] = lax.optimization_barrier((x_sample, w["w_mlp_in"]))
    y_s = _trunk(x_sample.reshape(dec_batch * dec_seq, d), mod[1:n_cond], dec_seq, w,
                 n_heads, qk, lam_init, rope=_rope_tables(dec_seq, qk), cache=cache)[0]

    return (y_p.reshape(batch, seq, d),
            y_s.reshape(dec_batch, dec_seq, d),
            new_k.reshape(batch, 1, seq, n_heads, 2, qk),
            new_v.reshape(batch, 1, seq, n_heads, 2 * qk))
```

```python
import functools
import math

import jax
import jax.numpy as jnp
from jax import lax
from jax.experimental import pallas as pl
from jax.experimental.pallas import tpu as pltpu

_GRID_W = 64
_ROPE_THETA = 10000.0
_EPS = 1e-6
_N_MOD = 6
_LANES = 128
_VMEM_LIMIT = 63 * 1024 * 1024

_F32 = jnp.float32
_BF16 = jnp.bfloat16


def _tile(dim, pref):
    t = min(dim, pref)
    assert dim % t == 0, (dim, pref)
    return t


def _params(*sem):
    return pltpu.CompilerParams(dimension_semantics=sem, vmem_limit_bytes=_VMEM_LIMIT)


class _SideCast:
    def __init__(self, src, n_i, n_j, col_block=0, n_col_blocks=1, row_part=0, n_row_parts=1):
        rows, cols = src.shape[0] // n_row_parts, src.shape[1] // n_col_blocks
        n_blocks = 1
        while n_blocks * 2 <= n_i * n_j and rows % (n_blocks * 2) == 0 and rows // (n_blocks * 2) >= 16:
            n_blocks *= 2
        self.src, self.n_j, self.n_blocks, self.all_steps = src, n_j, n_blocks, n_blocks == n_i * n_j
        block = (rows // n_blocks, cols)

        def row_block(*idx):
            step = idx[0] * n_j + idx[1] if len(idx) > 1 else idx[0]
            return jnp.minimum(step, n_blocks - 1)

        self.spec = pl.BlockSpec(block, lambda *idx: (row_part * n_blocks + row_block(*idx), col_block))
        self.out_spec = pl.BlockSpec(block, lambda *idx: (row_block(*idx), 0))
        self.out_shape = jax.ShapeDtypeStruct((rows, cols), _BF16)

    def run(self, src_ref, dst_ref):
        def cast():
            dst_ref[...] = src_ref[...].astype(dst_ref.dtype)
        if self.all_steps:
            cast()
        else:
            pl.when(pl.program_id(0) * self.n_j + pl.program_id(1) < self.n_blocks)(cast)


def _mod_kernel(cond_ref, w_ref, b_ref, o_ref):
    c = cond_ref[...]
    s = c / (1.0 + jnp.exp(-c))
    o_ref[...] = jnp.dot(s, w_ref[...], preferred_element_type=_F32) + b_ref[...]


def _modulation(cond, w_ada, b_ada):
    r, d = cond.shape
    n = w_ada.shape[1]
    tn = _tile(n, 512)
    return pl.pallas_call(
        _mod_kernel,
        grid=(n // tn,),
        in_specs=[pl.BlockSpec((r, d), lambda j: (0, 0)),
                  pl.BlockSpec((d, tn), lambda j: (0, j)),
                  pl.BlockSpec((1, tn), lambda j: (0, j))],
        out_specs=pl.BlockSpec((r, tn), lambda j: (0, j)),
        out_shape=jax.ShapeDtypeStruct((r, n), _F32),
        compiler_params=_params("arbitrary"),
        name="modulation",
    )(cond, w_ada, b_ada.reshape(1, n))


def _prenorm_kernel(x_ref, g_ref, mod_ref, o_ref, *, shift_idx, scale_idx):
    x = x_ref[...]
    gain = g_ref[...] * (1.0 + mod_ref[scale_idx:scale_idx + 1, :])
    shift = mod_ref[shift_idx:shift_idx + 1, :]
    r = lax.rsqrt(jnp.mean(x * x, axis=-1, keepdims=True) + _EPS)
    o_ref[...] = (x * r * gain + shift).astype(o_ref.dtype)


def _prenorm(x, g, mod3, shift_idx, scale_idx):
    r, d = x.shape
    rows_per_group = r // mod3.shape[0]
    tr = _tile(rows_per_group, 512)
    return pl.pallas_call(
        functools.partial(_prenorm_kernel, shift_idx=shift_idx, scale_idx=scale_idx),
        grid=(r // tr,),
        in_specs=[pl.BlockSpec((tr, d), lambda i: (i, 0)),
                  pl.BlockSpec((1, d), lambda i: (0, 0)),
                  pl.BlockSpec((None, _N_MOD, d), lambda i: ((i * tr) // rows_per_group, 0, 0))],
        out_specs=pl.BlockSpec((tr, d), lambda i: (i, 0)),
        out_shape=jax.ShapeDtypeStruct((r, d), _BF16),
        compiler_params=_params("parallel"),
        name="prenorm",
    )(x, g.reshape(1, d), mod3)


def _headnorm(xc, g, scale):
    r = lax.rsqrt(jnp.mean(xc * xc, axis=-1, keepdims=True) + _EPS)
    if scale != 1.0:
        r = r * scale
    return xc * r * g


def _rope(y, cos, sin_signed, first_half):
    qk = y.shape[-1]
    ahead = pltpu.roll(y, qk - qk // 4, axis=1)
    behind = pltpu.roll(y, qk // 4, axis=1)
    return y * cos + jnp.where(first_half, ahead, behind) * sin_signed


def _qkv_kernel(*refs, latent, qk, n_sec, q_scale, side_cast):
    if latent:
        x_ref, w_ref, qg_ref, kg_ref, cos_ref, sin_ref, qkv_ref = refs
    else:
        n_cast = len(side_cast)
        x_ref, w_ref, qg_ref, kg_ref = refs[:4]
        qkv_ref, newk_ref, newv_ref = refs[4 + n_cast:7 + n_cast]
        for cast, src_ref, dst_ref in zip(side_cast, refs[4:4 + n_cast], refs[7 + n_cast:]):
            cast.run(src_ref, dst_ref)
    j = pl.program_id(1)
    tn = w_ref.shape[1]

    def project():
        return jnp.dot(x_ref[...], w_ref[...], preferred_element_type=_F32)

    if latent:
        shared = project()
        project = lambda: shared

    def qk_epilogue(g_ref, scale, f32_ref):
        acc = project()
        g = g_ref[...]
        if latent:
            cos = cos_ref[...]
            sin = sin_ref[...]
            lane = lax.broadcasted_iota(jnp.int32, (1, qk), 1)
            first_half = (lane % (qk // 2)) < (qk // 4)
        for c in range(tn // qk):
            cols = slice(c * qk, (c + 1) * qk)
            y = _headnorm(acc[:, cols], g, scale)
            if latent:
                y = _rope(y, cos, sin, first_half)
            if f32_ref is not None:
                n_chunks = n_sec * (tn // qk)
                chunk = (j - n_sec) * (tn // qk) + c
                f32_ref[pl.ds(chunk, acc.shape[0], stride=n_chunks), :] = y
            qkv_ref[:, cols] = y.astype(qkv_ref.dtype)

    @pl.when(j < n_sec)
    def _():
        qk_epilogue(qg_ref, q_scale, None)

    @pl.when((j >= n_sec) & (j < 2 * n_sec))
    def _():
        qk_epilogue(kg_ref, 1.0, None if latent else newk_ref)

    @pl.when(j >= 2 * n_sec)
    def _():
        acc = project()
        qkv_ref[...] = acc.astype(qkv_ref.dtype)
        if not latent:
            newv_ref[...] = acc


def _qkv_proj(h, w_qkv, q_g, k_g, d_att, qk, rope=None, side_cast_args=None):
    r, d = h.shape
    latent = rope is not None
    tm = _tile(r if not latent else rope[0].shape[0], 1024)
    tn = _tile(d_att, 1024 if latent else 512)
    n_sec = d_att // tn
    in_specs = [pl.BlockSpec((tm, d), lambda i, j: (i, 0)),
                pl.BlockSpec((d, tn), lambda i, j: (0, j)),
                pl.BlockSpec((1, qk), lambda i, j: (0, 0)),
                pl.BlockSpec((1, qk), lambda i, j: (0, 0))]
    args = [h, w_qkv, q_g.reshape(1, qk), k_g.reshape(1, qk)]
    qkv_spec = pl.BlockSpec((tm, tn), lambda i, j: (i, j))
    qkv_shape = jax.ShapeDtypeStruct((r, 3 * d_att), _BF16)
    if latent:
        seq_tiles = rope[0].shape[0] // tm
        rope_spec = pl.BlockSpec((tm, qk), lambda i, j: (i % seq_tiles, 0))
        in_specs += [rope_spec, rope_spec]
        args += list(rope)
        out_specs, out_shape = qkv_spec, qkv_shape
        side_cast = None
    else:
        side_cast = [_SideCast(a[0], r // tm, 3 * n_sec, *a[1:]) for a in side_cast_args]
        in_specs += [c.spec for c in side_cast]
        args += [c.src for c in side_cast]
        n_chunks = d_att // qk
        out_specs = [qkv_spec,
                     pl.BlockSpec((tm * n_chunks, qk), lambda i, j: (i, 0)),
                     pl.BlockSpec((tm, tn), lambda i, j: (i, jnp.clip(j - 2 * n_sec, 0, n_sec - 1)))]
        out_specs += [c.out_spec for c in side_cast]
        out_shape = [qkv_shape,
                     jax.ShapeDtypeStruct((r * n_chunks, qk), _F32),
                     jax.ShapeDtypeStruct((r, d_att), _F32)]
        out_shape += [c.out_shape for c in side_cast]
    return pl.pallas_call(
        functools.partial(_qkv_kernel, latent=latent, qk=qk, n_sec=n_sec, q_scale=qk ** -0.5,
                          side_cast=side_cast),
        grid=(r // tm, 3 * n_sec),
        in_specs=in_specs, out_specs=out_specs, out_shape=out_shape,
        compiler_params=_params("parallel", "arbitrary"),
        name="qkv_proj_latent" if latent else "qkv_proj_context",
    )(*args)


def _conv_kernel(x_ref, wb_ref, wc_ref, wu_ref, cw_ref, *rest, seq, side_cast):
    if side_cast is None:
        o_ref, = rest
    else:
        src_ref, o_ref, dst_ref = rest
        side_cast.run(src_ref, dst_ref)
    x = x_ref[...]
    b = jnp.dot(x, wb_ref[...], preferred_element_type=_F32)
    c = jnp.dot(x, wc_ref[...], preferred_element_type=_F32)
    u = jnp.dot(x, wu_ref[...], preferred_element_type=_F32)
    z = c * u
    tm = z.shape[0]
    pos = lax.broadcasted_iota(jnp.int32, (tm, 1), 0) % seq
    z_prev = jnp.where(pos == 0, 0.0, pltpu.roll(z, 1, axis=0))
    z_next = jnp.where(pos == seq - 1, 0.0, pltpu.roll(z, tm - 1, axis=0))
    cw = cw_ref[...]
    y = cw[0:1, :] * z_prev + cw[1:2, :] * z + cw[2:3, :] * z_next
    o_ref[...] = (b * y).astype(o_ref.dtype)


def _conv_proj(h, w_conv, conv_w, seq, cast_src=None):
    r, d = h.shape
    d_conv = conv_w.shape[1]
    tm = _tile(r, max(seq, 1024))
    assert tm % seq == 0
    tc = _tile(d_conv, 256 if cast_src is not None else 512)
    nc = d_conv // tc

    def w_spec(sec):
        return pl.BlockSpec((d, tc), lambda i, j: (0, sec * nc + j))

    in_specs = [pl.BlockSpec((tm, d), lambda i, j: (i, 0)),
                w_spec(0), w_spec(1), w_spec(2),
                pl.BlockSpec((conv_w.shape[0], tc), lambda i, j: (0, j))]
    args = [h, w_conv, w_conv, w_conv, conv_w]
    out_specs = pl.BlockSpec((tm, tc), lambda i, j: (i, j))
    out_shape = jax.ShapeDtypeStruct((r, d_conv), _BF16)
    side_cast = None
    if cast_src is not None:
        side_cast = _SideCast(cast_src, r // tm, nc)
        in_specs.append(side_cast.spec)
        args.append(cast_src)
        out_specs, out_shape = [out_specs, side_cast.out_spec], [out_shape, side_cast.out_shape]
    return pl.pallas_call(
        functools.partial(_conv_kernel, seq=seq, side_cast=side_cast),
        grid=(r // tm, nc),
        in_specs=in_specs, out_specs=out_specs, out_shape=out_shape,
        compiler_params=_params("parallel", "arbitrary"),
        name="conv_proj",
    )(*args)


def _attn_kernel(*refs, n_heads, qk, lam_init, cached, side_cast):
    hw = 2 * qk
    if cached:
        q_ref, k_ref, v_ref, ck_ref, cv_ref, lam_ref, g_ref, o_ref, ckb_ref, cvb_ref = refs
        past = cv_ref.shape[0]

        @pl.when(pl.program_id(1) == 0)
        def _():
            for h in range(n_heads):
                for c in range(2):
                    rows = pl.ds(2 * h + c, past, stride=2 * n_heads)
                    ckb_ref[:, h * hw + c * qk:h * hw + (c + 1) * qk] = ck_ref[rows, :].astype(_BF16)
                cvb_ref[:, h * hw:(h + 1) * hw] = cv_ref[:, h, :].astype(_BF16)
    elif side_cast is not None:
        q_ref, k_ref, v_ref, lam_ref, g_ref, src_ref, o_ref, dst_ref = refs
        side_cast.run(src_ref, dst_ref)
    else:
        q_ref, k_ref, v_ref, lam_ref, g_ref, o_ref = refs
    lv = lam_ref[...]
    lam = (jnp.exp(jnp.sum(lv[0:1] * lv[1:2], axis=-1, keepdims=True))
           - jnp.exp(jnp.sum(lv[2:3] * lv[3:4], axis=-1, keepdims=True)) + lam_init)
    g = g_ref[...]
    dims = (((1,), (1,)), ((), ()))

    def probs(q, keys):
        s = [lax.dot_general(q, k, dims, preferred_element_type=_F32) for k in keys]
        m = functools.reduce(jnp.maximum, [jnp.max(x, axis=-1, keepdims=True) for x in s])
        e = [jnp.exp(x - m) for x in s]
        den = functools.reduce(jnp.add, [jnp.sum(x, axis=-1, keepdims=True) for x in e])
        return e, 1.0 / den

    for h in range(n_heads):
        o = None
        c0, c1 = h * hw, h * hw + qk
        keys0 = [k_ref[:, c0:c0 + qk]]
        keys1 = [k_ref[:, c1:c1 + qk]]
        vals = [v_ref[:, c0:c0 + hw]]
        if cached:
            keys0.append(ckb_ref[:, c0:c0 + qk])
            keys1.append(ckb_ref[:, c1:c1 + qk])
            vals.append(cvb_ref[:, c0:c0 + hw])
        e0, r0 = probs(q_ref[:, c0:c0 + qk], keys0)
        e1, r1 = probs(q_ref[:, c1:c1 + qk], keys1)
        r1 = r1 * lam
        for p0, p1, v in zip(e0, e1, vals):
            a = (p0 * r0 - p1 * r1).astype(_BF16)
            part = jnp.dot(a, v, preferred_element_type=_F32)
            o = part if o is None else o + part
        o = o * lax.rsqrt(jnp.mean(o * o, axis=-1, keepdims=True) + _EPS) * g * (1.0 - lam_init)
        o_ref[:, c0:c0 + hw] = o.astype(o_ref.dtype)


def _attention(qkv, lam_vecs, subln_g, n_heads, qk, seq, lam_init, cache=None, cast_src=None):
    r = qkv.shape[0]
    d_att = n_heads * 2 * qk
    tq = _tile(seq, 512 if cache is not None else 256)
    q_tiles = seq // tq
    in_specs = [pl.BlockSpec((tq, d_att), lambda b, t: (b * q_tiles + t, 0)),
                pl.BlockSpec((seq, d_att), lambda b, t: (b, 1)),
                pl.BlockSpec((seq, d_att), lambda b, t: (b, 2))]
    args = [qkv, qkv, qkv]
    if cache is not None:
        ck, cv = cache
        in_specs += [pl.BlockSpec((None,) + ck.shape[1:], lambda b, t: (b, 0, 0)),
                     pl.BlockSpec((None,) + cv.shape[1:], lambda b, t: (b, 0, 0, 0))]
        args += [ck, cv]
        scratch = [pltpu.VMEM((cv.shape[1], d_att), _BF16)] * 2
    else:
        scratch = []
    in_specs += [pl.BlockSpec(lam_vecs.shape, lambda b, t: (0, 0)),
                 pl.BlockSpec((1, 2 * qk), lambda b, t: (0, 0))]
    args += [lam_vecs, subln_g.reshape(1, 2 * qk)]
    out_specs = pl.BlockSpec((tq, d_att), lambda b, t: (b * q_tiles + t, 0))
    out_shape = jax.ShapeDtypeStruct((r, d_att), _BF16)
    side_cast = None
    if cast_src is not None:
        side_cast = _SideCast(cast_src, r // seq, q_tiles)
        in_specs.append(side_cast.spec)
        args.append(cast_src)
        out_specs, out_shape = [out_specs, side_cast.out_spec], [out_shape, side_cast.out_shape]
    return pl.pallas_call(
        functools.partial(_attn_kernel, n_heads=n_heads, qk=qk, lam_init=lam_init,
                          cached=cache is not None, side_cast=side_cast),
        grid=(r // seq, q_tiles),
        in_specs=in_specs, out_specs=out_specs, out_shape=out_shape,
        scratch_shapes=scratch,
        compiler_params=_params("parallel", "arbitrary"),
        name="diff_attention_latent" if cache is not None else "diff_attention_context",
    )(*args)


def _outproj_kernel(a_ref, c_ref, wa_ref, wc_ref, x_ref, mod_ref, *rest, gate_idx, side_cast):
    if side_cast is None:
        o_ref, = rest
    else:
        src_ref, o_ref, dst_ref = rest
        side_cast.run(src_ref, dst_ref)
    y = (jnp.dot(a_ref[...], wa_ref[...], preferred_element_type=_F32)
         + jnp.dot(c_ref[...], wc_ref[...], preferred_element_type=_F32))
    o_ref[...] = x_ref[...] + mod_ref[gate_idx:gate_idx + 1, :] * y


def _out_proj(attn, conv, w_out, x, mod3, gate_idx, cast_src=None):
    r, d = x.shape
    d_att, d_conv = attn.shape[1], conv.shape[1]
    assert d_att == d_conv
    rows_per_group = r // mod3.shape[0]
    tm = _tile(rows_per_group, 1024)
    tn = _tile(d, 512 if cast_src is not None else 1024)
    in_specs = [pl.BlockSpec((tm, d_att), lambda i, j: (i, 0)),
                pl.BlockSpec((tm, d_conv), lambda i, j: (i, 0)),
                pl.BlockSpec((d_att, tn), lambda i, j: (0, j)),
                pl.BlockSpec((d_conv, tn), lambda i, j: (1, j)),
                pl.BlockSpec((tm, tn), lambda i, j: (i, j)),
                pl.BlockSpec((None, _N_MOD, tn), lambda i, j: ((i * tm) // rows_per_group, 0, j))]
    args = [attn, conv, w_out, w_out, x, mod3]
    out_specs = pl.BlockSpec((tm, tn), lambda i, j: (i, j))
    out_shape = jax.ShapeDtypeStruct((r, d), _F32)
    side_cast = None
    if cast_src is not None:
        side_cast = _SideCast(cast_src, r // tm, d // tn, row_part=1, n_row_parts=2)
        in_specs.append(side_cast.spec)
        args.append(cast_src)
        out_specs, out_shape = [out_specs, side_cast.out_spec], [out_shape, side_cast.out_shape]
    return pl.pallas_call(
        functools.partial(_outproj_kernel, gate_idx=gate_idx, side_cast=side_cast),
        grid=(r // tm, d // tn),
        in_specs=in_specs, out_specs=out_specs, out_shape=out_shape,
        compiler_params=_params("parallel", "arbitrary"),
        name="out_proj",
    )(*args)


def _mlp_kernel(h_ref, w1_ref, w2a_ref, w2b_ref, x_ref, mod_ref, o_ref, hid_ref, *, gate_idx, n_f, tf):
    s = pl.program_id(1)

    @pl.when(s < n_f)
    def _():
        hid = jnp.dot(h_ref[...], w1_ref[...], preferred_element_type=_F32)
        col = pl.multiple_of(s * tf, tf)
        hid_ref[:, pl.ds(col, tf)] = jnp.square(jnp.maximum(hid, 0.0)).astype(hid_ref.dtype)

    @pl.when(s >= n_f)
    def _():
        half = w2a_ref.shape[0]
        y = (jnp.dot(hid_ref[:, :half], w2a_ref[...], preferred_element_type=_F32)
             + jnp.dot(hid_ref[:, half:], w2b_ref[...], preferred_element_type=_F32))
        o_ref[...] = x_ref[...] + mod_ref[gate_idx:gate_idx + 1, :] * y


def _mlp(h, w1, w2_halves, x, mod3, gate_idx):
    r, d = x.shape
    d_ff = w1.shape[1]
    rows_per_group = r // mod3.shape[0]
    tm = _tile(rows_per_group, 512)
    tf = _tile(d_ff, 1024)
    tn = _tile(d, 256)
    n_f = d_ff // tf

    def out_col(s):
        return jnp.maximum(s - n_f, 0)

    return pl.pallas_call(
        functools.partial(_mlp_kernel, gate_idx=gate_idx, n_f=n_f, tf=tf),
        grid=(r // tm, n_f + d // tn),
        in_specs=[pl.BlockSpec((tm, d), lambda i, s: (i, 0)),
                  pl.BlockSpec((d, tf), lambda i, s: (0, jnp.minimum(s, n_f - 1))),
                  pl.BlockSpec((d_ff // 2, tn), lambda i, s: (0, out_col(s))),
                  pl.BlockSpec((d_ff // 2, tn), lambda i, s: (0, out_col(s))),
                  pl.BlockSpec((tm, tn), lambda i, s: (i, out_col(s))),
                  pl.BlockSpec((None, _N_MOD, tn),
                               lambda i, s: ((i * tm) // rows_per_group, 0, out_col(s)))],
        out_specs=pl.BlockSpec((tm, tn), lambda i, s: (i, out_col(s))),
        out_shape=jax.ShapeDtypeStruct((r, d), _F32),
        scratch_shapes=[pltpu.VMEM((tm, d_ff), _BF16)],
        compiler_params=_params("parallel", "arbitrary"),
        name="mlp",
    )(h, w1, *w2_halves, x, mod3)


def _rope_tables(n_tokens, qk):
    axis_dim = qk // 2
    rows = n_tokens // _GRID_W
    row = jnp.repeat(jnp.arange(rows, dtype=_F32), _GRID_W)
    col = jnp.tile(jnp.arange(_GRID_W, dtype=_F32), rows)
    inv = jnp.power(_ROPE_THETA, -jnp.arange(0, axis_dim, 2, dtype=_F32) / axis_dim)
    row_ang, col_ang = row[:, None] * inv, col[:, None] * inv
    cos = jnp.concatenate([jnp.cos(row_ang)] * 2 + [jnp.cos(col_ang)] * 2, axis=-1)
    sin = jnp.concatenate([-jnp.sin(row_ang), jnp.sin(row_ang),
                           -jnp.sin(col_ang), jnp.sin(col_ang)], axis=-1)
    return cos, sin


def _trunk(x, mod3, seq, w, n_heads, qk, lam_init, rope=None, cache=None):
    d_att = n_heads * 2 * qk
    context = rope is None
    h = _prenorm(x, w["norm_attn_g"], mod3, 0, 1)
    if context:
        w = dict(w)
        assert w["w_in"].shape[1] == 2 * w["w_qkv"].shape[1]
        qkv, new_k, new_v, w["w_conv"], w2_top = _qkv_proj(h, w["w_qkv"], w["q_norm_g"], w["k_norm_g"], d_att, qk,
                                                   side_cast_args=[(w["w_in"], 1, 2),
                                                                   (w["w_mlp_out"], 0, 1, 0, 2)])
        attn, w["w_out"] = _attention(qkv, w["lam_vecs"], w["subln_g"], n_heads, qk, seq, lam_init,
                                      cast_src=w["w_out"])
        h, attn = lax.optimization_barrier((h, attn))
        conv, w["w_mlp_in"] = _conv_proj(h, w["w_conv"], w["conv_w"], seq, cast_src=w["w_mlp_in"])
    else:
        qkv = _qkv_proj(h, w["w_qkv"], w["q_norm_g"], w["k_norm_g"], d_att, qk, rope)
        conv = _conv_proj(h, w["w_conv"], w["conv_w"], seq)
        attn = _attention(qkv, w["lam_vecs"], w["subln_g"], n_heads, qk, seq, lam_init, cache)
        new_k = new_v = None
    if context:
        x2, w2_bottom = _out_proj(attn, conv, w["w_out"], x, mod3, 2, cast_src=w["w_mlp_out"])
        w["w_mlp_out"] = (w2_top, w2_bottom)
    else:
        x2 = _out_proj(attn, conv, w["w_out"], x, mod3, 2)
    h2 = _prenorm(x2, w["norm_mlp_g"], mod3, 3, 4)
    y = _mlp(h2, w["w_mlp_in"], w["w_mlp_out"], x2, mod3, 5)
    return y, new_k, new_v, w


def kernel(x_prompt, x_sample, cache_k, cache_v, c, c_ctx, w_ada, b_ada, norm_attn_g, w_in, q_norm_g, k_norm_g, lambda_q1, lambda_k1, lambda_q2, lambda_k2, subln_g, conv_w, w_out, norm_mlp_g, w_mlp_in, w_mlp_out):
    assert w_in.shape[0] == 1, "one trunk layer"
    batch, seq, d = x_prompt.shape
    dec_batch, dec_seq, _ = x_sample.shape
    past, n_heads, _, qk = cache_k.shape[2:]
    d_att = n_heads * 2 * qk
    layer = 0
    lam_init = 0.8 - 0.6 * math.exp(-0.3 * layer)

    w = {
        "norm_attn_g": norm_attn_g[layer], "q_norm_g": q_norm_g[layer], "k_norm_g": k_norm_g[layer],
        "subln_g": subln_g[layer], "conv_w": conv_w[layer], "norm_mlp_g": norm_mlp_g[layer],
        "lam_vecs": jnp.stack([lambda_q1[layer], lambda_k1[layer], lambda_q2[layer], lambda_k2[layer]]),
        "w_in": w_in[layer], "w_qkv": w_in[layer][:, :3 * d_att].astype(_BF16), "w_out": w_out[layer],
        "w_mlp_in": w_mlp_in[layer], "w_mlp_out": w_mlp_out[layer],
    }

    n_cond = 1 + dec_batch
    cond = jnp.concatenate([c_ctx[None], c, jnp.zeros((-n_cond % 8, d), _F32)], axis=0)
    mod = _modulation(cond, w_ada[layer], b_ada[layer]).reshape(cond.shape[0], _N_MOD, d)

    y_p, new_k, new_v, w = _trunk(x_prompt.reshape(batch * seq, d), mod[0:1], seq, w,
                                  n_heads, qk, lam_init)
    cache = (cache_k[:, layer].reshape(dec_batch, past * n_heads * 2, qk), cache_v[:, layer])
    x_sample, w["w_mlp_in"] = lax.optimization_barrier((x_sample, w["w_mlp_in"]))
    y_s = _trunk(x_sample.reshape(dec_batch * dec_seq, d), mod[1:n_cond], dec_seq, w,
                 n_heads, qk, lam_init, rope=_rope_tables(dec_seq, qk), cache=cache)[0]

    return (y_p.reshape(batch, seq, d),
            y_s.reshape(dec_batch, dec_seq, d),
            new_k.reshape(batch, 1, seq, n_heads, 2, qk),
            new_v.reshape(batch, 1, seq, n_heads, 2 * qk))
```
